```python
import math
import jax
import jax.numpy as jnp
from jax import lax
import numpy as np

D_MODEL = 4096
BATCH = 4
SEQ = 2048
DEPTH = 4
DEC_BATCH = 128
DEC_SEQ = 1
PAST_LEN = 16384
PAGE_SIZE = 128

N_META = 16
CHUNK = 64
N_BRANCH = 3
BRANCH_WIDTH = D_MODEL // 2
GDN_HEADS = 16
GDN_DK = BRANCH_WIDTH // GDN_HEADS
GDN_DV = BRANCH_WIDTH // GDN_HEADS
GDN_QK = GDN_HEADS * GDN_DK
GDN_V = GDN_HEADS * GDN_DV
GDN_CONV_CH = 2 * GDN_QK + GDN_V
CONV_WIDTH = 4
HG_HEADS = 16
HG_DK = BRANCH_WIDTH // HG_HEADS
HG_DV = BRANCH_WIDTH // HG_HEADS
HG_K = HG_HEADS * HG_DK
HG_V = HG_HEADS * HG_DV
ML_HEADS = 4
ML_DV = BRANCH_WIDTH // ML_HEADS
ML_DK = ML_DV // 2
ML_K = ML_HEADS * ML_DK
ML_V = ML_HEADS * ML_DV
GATE_SOFTCAP = 15.0
D_FF = 4 * D_MODEL
EPS = 1e-6
TINY = 1e-30
NEG_BIG = -1e30
IN_SIZES = (GDN_CONV_CH, GDN_V, GDN_HEADS, GDN_HEADS,
            HG_K, HG_K, HG_V, HG_V,
            ML_K, ML_K, ML_V, ML_V, ML_HEADS, ML_HEADS,
            N_BRANCH * D_MODEL)
IN_COLS = sum(IN_SIZES)

kernel_name = 'hybrid_gdn_hgrn2_mlstm_decoder_step'


def split_points(sizes):
    pts, acc = [], 0
    for s in sizes[:-1]:
        acc += s
        pts.append(acc)
    return pts


def rms_norm(x, g):
    xf = x.astype(jnp.float32)
    y = xf * lax.rsqrt(jnp.mean(xf * xf, axis=-1, keepdims=True) + EPS)
    return (y * g.astype(jnp.float32)).astype(x.dtype)


def l2norm(x):
    return x * lax.rsqrt(jnp.sum(x * x, axis=-1, keepdims=True) + EPS)


def softcap(x):
    return GATE_SOFTCAP * jnp.tanh(x / GATE_SOFTCAP)


def masked_exp(diff, mask):
    return jnp.where(mask, jnp.exp(jnp.where(mask, diff, 0.0)), 0.0)


def to_heads(x, d):
    b, t, _ = x.shape
    return x.reshape(b, t, -1, d).transpose(0, 2, 1, 3)


def from_heads(x):
    return x.transpose(0, 2, 1, 3)


def causal_conv(x, buf, w):
    xp = jnp.concatenate([buf.astype(x.dtype), x], axis=1)
    y = lax.conv_general_dilated(xp, w.astype(x.dtype)[:, None, :], window_strides=(1,), padding='VALID',
                                 dimension_numbers=('NWC', 'WIO', 'NWC'), feature_group_count=x.shape[-1])
    return y, xp[:, -(CONV_WIDTH - 1):]


def run_chunked(chunk_fn, state, seqs, lead):
    outs = []
    if lead:
        state, o = chunk_fn(state, tuple(a[:, :, :lead] for a in seqs))
        outs.append(o)
        seqs = tuple(a[:, :, lead:] for a in seqs)
    t = seqs[0].shape[2]
    n_full, rem = divmod(t, CHUNK)
    if n_full:
        def blocks(a):
            a = a[:, :, :n_full * CHUNK]
            a = a.reshape(a.shape[:2] + (n_full, CHUNK) + a.shape[3:])
            return jnp.moveaxis(a, 2, 0)
        state, o = lax.scan(chunk_fn, state, tuple(blocks(a) for a in seqs))
        o = jnp.moveaxis(o, 0, 2)
        outs.append(o.reshape(o.shape[:2] + (n_full * CHUNK,) + o.shape[4:]))
    if rem:
        state, o = chunk_fn(state, tuple(a[:, :, n_full * CHUNK:] for a in seqs))
        outs.append(o)
    return state, jnp.concatenate(outs, axis=2)


def gdn_chunk(s, inp):
    q, k, v, g, beta = inp
    L = q.shape[2]
    gc = jnp.cumsum(g, axis=-1)
    causal = jnp.tril(jnp.ones((L, L), bool))
    strict = jnp.tril(jnp.ones((L, L), bool), -1)
    decay = masked_exp(gc[..., :, None] - gc[..., None, :], causal)
    kb = k * beta[..., None]
    a_mat = jnp.eye(L, dtype=q.dtype) + jnp.where(strict, jnp.einsum('bhtd,bhsd->bhts', kb, k) * decay, 0.0)
    rhs = jnp.concatenate([v * beta[..., None], kb * jnp.exp(gc)[..., None]], axis=-1)
    sol = lax.linalg.triangular_solve(a_mat, rhs, left_side=True, lower=True, unit_diagonal=True)
    u, w = sol[..., :v.shape[-1]], sol[..., v.shape[-1]:]
    v_new = u - jnp.einsum('bhtk,bhkv->bhtv', w, s)
    attn = jnp.einsum('bhtd,bhsd->bhts', q, k) * decay
    o = (jnp.einsum('bhtk,bhkv->bhtv', q * jnp.exp(gc)[..., None], s)
         + jnp.einsum('bhts,bhsv->bhtv', attn, v_new))
    g_last = gc[..., -1]
    s = (s * jnp.exp(g_last)[..., None, None]
         + jnp.einsum('bhsk,bhsv->bhkv', k * jnp.exp(g_last[..., None] - gc)[..., None], v_new))
    return s, o


def hgrn_chunk(s, inp):
    q, k, v, lf = inp
    L = q.shape[2]
    b = jnp.cumsum(lf, axis=2)
    causal = jnp.tril(jnp.ones((L, L), bool))[..., None]
    w = masked_exp(b[:, :, :, None, :] - b[:, :, None, :, :], causal)
    attn = jnp.einsum('bhtd,bhsd,bhtsd->bhts', q, k, w)
    o = (jnp.einsum('bhtk,bhkv->bhtv', q * jnp.exp(b), s)
         + jnp.einsum('bhts,bhsv->bhtv', attn, v))
    b_last = b[:, :, -1]
    s = (s * jnp.exp(b_last)[..., None]
         + jnp.einsum('bhsk,bhsv->bhkv', k * jnp.exp(b_last[:, :, None] - b), v))
    return s, o


def mlstm_chunk(state, inp):
    c, n, m = state
    q, k, v, li, lf = inp
    L = q.shape[2]
    b = jnp.cumsum(lf, axis=-1)
    causal = jnp.tril(jnp.ones((L, L), bool))
    d = b[..., :, None] - b[..., None, :] + li[..., None, :]
    inter = b + m[..., None]
    m_t = jnp.maximum(inter, jnp.max(jnp.where(causal, d, NEG_BIG), axis=-1))
    dw = masked_exp(d - m_t[..., None], causal)
    iw = jnp.exp(inter - m_t)
    qk = jnp.einsum('bhtd,bhsd->bhts', q, k) * dw
    num = (iw[..., None] * jnp.einsum('bhtk,bhkv->bhtv', q, c)
           + jnp.einsum('bhts,bhsv->bhtv', qk, v))
    den = iw * jnp.einsum('bhtk,bhk->bht', q, n) + jnp.sum(qk, axis=-1)
    h = num / jnp.maximum(jnp.abs(den), jnp.exp(-m_t))[..., None]
    b_last = b[..., -1]
    d_last = b_last[..., None] - b + li
    m_new = jnp.maximum(b_last + m, jnp.max(d_last, axis=-1))
    w_last = jnp.exp(d_last - m_new[..., None])
    dec = jnp.exp(b_last + m - m_new)
    c = dec[..., None, None] * c + jnp.einsum('bhsk,bhsv->bhkv', k * w_last[..., None], v)
    n = dec[..., None] * n + jnp.einsum('bhsk,bhs->bhk', k, w_last)
    return (c, n, m_new), h


def trunk_layer(h, lead, states, params, lb):
    g_s, g_conv, hg_s, ml_c, ml_n, ml_m = states
    (n1, w_in, conv_w, a_log, dt_bias, gdn_ng, hg_ng, ml_ib, ml_fb, ml_ng,
     w_br, w_o, n2, w_up, w_dn) = params
    f32 = jnp.float32
    bsz, t, _ = h.shape
    proj = rms_norm(h, n1) @ w_in
    (qkv, z, a, b, hq, hf, hi, hgate, mq, mk, mv, mo, mi, mf, gates) = jnp.split(
        proj, split_points(IN_SIZES), axis=-1)

    qkv, new_conv = causal_conv(qkv, g_conv, conv_w)
    qkv = jax.nn.silu(qkv.astype(f32))
    q, k, v = jnp.split(qkv, [GDN_QK, 2 * GDN_QK], axis=-1)
    q = l2norm(to_heads(q, GDN_DK)) * GDN_DK ** -0.5
    k = l2norm(to_heads(k, GDN_DK))
    v = to_heads(v, GDN_DV)
    beta = jax.nn.sigmoid(b.astype(f32)).transpose(0, 2, 1)
    g = (-jnp.exp(a_log.astype(f32)) * jax.nn.softplus(a.astype(f32) + dt_bias.astype(f32))).transpose(0, 2, 1)
    new_gs, o = run_chunked(gdn_chunk, g_s.astype(f32), (q, k, v, g, beta), lead)
    zg = jax.nn.silu(z.astype(f32)).reshape(bsz, t, GDN_HEADS, GDN_DV)
    o_gdn = (rms_norm(from_heads(o), gdn_ng) * zg).reshape(bsz, t, BRANCH_WIDTH)

    lbh = lb.reshape(1, HG_HEADS, 1, HG_DK)
    q = to_heads(hq.astype(f32), HG_DK)
    hf_h = to_heads(hf.astype(f32), HG_DK)
    f = lbh + (1.0 - lbh) * jax.nn.sigmoid(hf_h)
    lf = jnp.log(jnp.maximum(f, TINY))
    k = (1.0 - lbh) * jax.nn.sigmoid(-hf_h)
    v = to_heads(hi.astype(f32), HG_DV)
    new_hs, o = run_chunked(hgrn_chunk, hg_s.astype(f32), (q, k, v, lf), lead)
    og = jax.nn.silu(hgate.astype(f32)).reshape(bsz, t, HG_HEADS, HG_DV)
    o_hg = (rms_norm(from_heads(o), hg_ng) * og).reshape(bsz, t, BRANCH_WIDTH)

    q = to_heads(mq.astype(f32), ML_DK) * ML_DK ** -0.5
    k = to_heads(mk.astype(f32), ML_DK)
    v = to_heads(mv.astype(f32), ML_DV)
    li = softcap(mi.astype(f32) + ml_ib.astype(f32)).transpose(0, 2, 1)
    lfm = jax.nn.log_sigmoid(softcap(mf.astype(f32) + ml_fb.astype(f32))).transpose(0, 2, 1)
    (new_c, new_n, new_m), o = run_chunked(
        mlstm_chunk, (ml_c.astype(f32), ml_n.astype(f32), ml_m.astype(f32)), (q, k, v, li, lfm), lead)
    om = jax.nn.sigmoid(mo.astype(f32)).reshape(bsz, t, ML_HEADS, ML_DV)
    o_ml = (rms_norm(from_heads(o), ml_ng) * om).reshape(bsz, t, BRANCH_WIDTH)

    br = jnp.stack([o_gdn, o_hg, o_ml], axis=0).astype(h.dtype)
    up = jnp.einsum('nbtc,ncd->btnd', br, w_br)
    gate = jax.nn.sigmoid(gates.astype(f32)).reshape(bsz, t, N_BRANCH, D_MODEL)
    merged = jnp.sum(gate * up.astype(f32), axis=2).astype(h.dtype)
    h = h + merged @ w_o

    h = h + jnp.square(jax.nn.relu(rms_norm(h, n2) @ w_up)) @ w_dn
    return h, (new_gs, new_conv, new_hs, new_c, new_n, new_m)


def setup_inputs(seed: int = 0) -> dict:
    key = jax.random.key(seed)
    ks = jax.random.split(key, 32)

    def nrm(i, shape, scale):
        return scale * jax.random.normal(ks[i], shape, jnp.float32)

    dt = jnp.exp(jax.random.uniform(ks[12], (DEPTH, GDN_HEADS), jnp.float32,
                                    minval=math.log(1e-3), maxval=math.log(1e-1)))
    return {
        'x_prompt': nrm(0, (BATCH, SEQ, D_MODEL), 1.0),
        'x_sample': nrm(1, (DEC_BATCH, DEC_SEQ, D_MODEL), 1.0),
        'state_gdn_s': nrm(2, (DEPTH, DEC_BATCH, GDN_HEADS, GDN_DK, GDN_DV), 0.1),
        'state_gdn_conv': nrm(3, (DEPTH, DEC_BATCH, CONV_WIDTH - 1, GDN_CONV_CH), 1.0),
        'state_hg_s': nrm(4, (DEPTH, DEC_BATCH, HG_HEADS, HG_DK, HG_DV), 0.3),
        'state_ml_c': nrm(5, (DEPTH, DEC_BATCH, ML_HEADS, ML_DK, ML_DV), 0.1),
        'state_ml_n': nrm(6, (DEPTH, DEC_BATCH, ML_HEADS, ML_DK), 0.3),
        'state_ml_m': nrm(7, (DEPTH, DEC_BATCH, ML_HEADS), 1.0),
        'meta_tokens': nrm(8, (N_META, D_MODEL), 1.0),
        'norm1_g': 1.0 + nrm(9, (DEPTH, D_MODEL), 0.02),
        'w_in': nrm(10, (DEPTH, D_MODEL, IN_COLS), D_MODEL ** -0.5),
        'gdn_conv_w': nrm(11, (DEPTH, CONV_WIDTH, GDN_CONV_CH), CONV_WIDTH ** -0.5),
        'gdn_a_log': jnp.log(jax.random.uniform(ks[13], (DEPTH, GDN_HEADS), jnp.float32, minval=1.0, maxval=16.0)),
        'gdn_dt_bias': dt + jnp.log(-jnp.expm1(-dt)),
        'gdn_norm_g': 1.0 + nrm(14, (DEPTH, GDN_DV), 0.02),
        'hg_lb_logits': nrm(15, (DEPTH, HG_K), 1.0),
        'hg_norm_g': 1.0 + nrm(16, (DEPTH, HG_DV), 0.02),
        'ml_i_bias': nrm(17, (DEPTH, ML_HEADS), 0.1),
        'ml_f_bias': 3.0 + nrm(18, (DEPTH, ML_HEADS), 0.5),
        'ml_norm_g': 1.0 + nrm(19, (DEPTH, ML_DV), 0.02),
        'w_branch': nrm(20, (DEPTH, N_BRANCH, BRANCH_WIDTH, D_MODEL), BRANCH_WIDTH ** -0.5),
        'w_out': nrm(21, (DEPTH, D_MODEL, D_MODEL), D_MODEL ** -0.5),
        'norm2_g': 1.0 + nrm(22, (DEPTH, D_MODEL), 0.02),
        'w_up': nrm(23, (DEPTH, D_MODEL, D_FF), D_MODEL ** -0.5),
        'w_down': nrm(24, (DEPTH, D_FF, D_MODEL), D_FF ** -0.5),
        'final_norm_g': 1.0 + nrm(25, (D_MODEL,), 0.02),
    }


def stack_states(per_layer):
    return tuple(jnp.stack([st[i] for st in per_layer], axis=0) for i in range(6))


def reference(x_prompt, x_sample, state_gdn_s, state_gdn_conv, state_hg_s, state_ml_c, state_ml_n,
              state_ml_m, meta_tokens, norm1_g, w_in, gdn_conv_w, gdn_a_log, gdn_dt_bias, gdn_norm_g,
              hg_lb_logits, hg_norm_g, ml_i_bias, ml_f_bias, ml_norm_g, w_branch, w_out, norm2_g,
              w_up, w_down, final_norm_g):
    f32 = jnp.float32
    act = x_prompt.dtype
    bp = x_prompt.shape[0]
    meta = jnp.broadcast_to(meta_tokens.astype(act)[None], (bp, N_META, D_MODEL))
    h_p = jnp.concatenate([meta, x_prompt], axis=1)
    h_s = x_sample
    lb_w = jax.nn.softmax(hg_lb_logits.astype(f32), axis=0)
    lb_all = jnp.cumsum(lb_w, axis=0) - lb_w[0]
    zero_states = (
        jnp.zeros((bp, GDN_HEADS, GDN_DK, GDN_DV), f32),
        jnp.zeros((bp, CONV_WIDTH - 1, GDN_CONV_CH), act),
        jnp.zeros((bp, HG_HEADS, HG_DK, HG_DV), f32),
        jnp.zeros((bp, ML_HEADS, ML_DK, ML_DV), f32),
        jnp.zeros((bp, ML_HEADS, ML_DK), f32),
        jnp.zeros((bp, ML_HEADS), f32),
    )
    cache_in = (state_gdn_s, state_gdn_conv, state_hg_s, state_ml_c, state_ml_n, state_ml_m)
    new_p, new_s = [], []
    for l in range(DEPTH):
        params = (norm1_g[l], w_in[l], gdn_conv_w[l], gdn_a_log[l], gdn_dt_bias[l], gdn_norm_g[l],
                  hg_norm_g[l], ml_i_bias[l], ml_f_bias[l], ml_norm_g[l], w_branch[l], w_out[l],
                  norm2_g[l], w_up[l], w_down[l])
        h_p, sp = trunk_layer(h_p, N_META, zero_states, params, lb_all[l])
        h_s, ss = trunk_layer(h_s, 0, tuple(c[l] for c in cache_in), params, lb_all[l])
        new_p.append(sp)
        new_s.append(ss)
    y_prompt = rms_norm(h_p, final_norm_g)[:, N_META:]
    y_sample = rms_norm(h_s, final_norm_g)
    p_gdn_s, p_gdn_conv, p_hg_s, p_ml_c, p_ml_n, p_ml_m = stack_states(new_p)
    s_gdn_s, s_gdn_conv, s_hg_s, s_ml_c, s_ml_n, s_ml_m = stack_states(new_s)
    return (y_prompt, y_sample, p_gdn_s, p_gdn_conv, p_hg_s, p_ml_c, p_ml_n, p_ml_m,
            s_gdn_s, s_gdn_conv, s_hg_s, s_ml_c, s_ml_n, s_ml_m)
```

```python
import functools
import math

import numpy as np
import jax
import jax.numpy as jnp
from jax import lax
from jax.experimental import pallas as pl
from jax.experimental.pallas import tpu as pltpu

F32 = jnp.float32
BF16 = jnp.bfloat16
HIGHEST = lax.Precision.HIGHEST

EPS = 1e-6
TINY = 1e-30
NEG_BIG = -1e30
GATE_SOFTCAP = 15.0

LANES = 128
SUBLANES = 8
BF16_ROWS = 16
VMEM_LIMIT = 56 * 1024 * 1024

ROW_ALIGN = 128
TM_CAP = 1056
TN_CAP = 1024
TK_CAP = 4096
NORM_ROWS_CAP = 256
CHUNK = 64
SAMPLE_ROWS = 16
ML_SAMPLE_ROWS = 8


def _largest_divisor(n, cap, mult):
    best = None
    for d in range(mult, min(n, cap) + 1, mult):
        if n % d == 0:
            best = d
    assert best is not None, (n, cap, mult)
    return best


def _round_up(n, m):
    return (n + m - 1) // m * m


def _cparams(sem):
    return pltpu.CompilerParams(dimension_semantics=sem, vmem_limit_bytes=VMEM_LIMIT)


def _dot(a, b):
    return jnp.dot(a.astype(BF16), b.astype(BF16), preferred_element_type=F32)


def _dot_nt(a, b):
    return lax.dot_general(a.astype(BF16), b.astype(BF16), (((1,), (1,)), ((), ())),
                           preferred_element_type=F32)


def _dot_tn(a, b):
    return lax.dot_general(a.astype(BF16), b.astype(BF16), (((0,), (0,)), ((), ())),
                           preferred_element_type=F32)


def _dot_hi(a, b):
    return jnp.dot(a, b, precision=HIGHEST, preferred_element_type=F32)


def _dot_nt_hi(a, b):
    return lax.dot_general(a, b, (((1,), (1,)), ((), ())), precision=HIGHEST,
                           preferred_element_type=F32)


def _sigmoid(x):
    return 1.0 / (1.0 + jnp.exp(-x))


def _silu(x):
    return x * _sigmoid(x)


def _softplus(x):
    return jnp.maximum(x, 0.0) + jnp.log1p(jnp.exp(-jnp.abs(x)))


def _log_sigmoid(x):
    return -_softplus(-x)


def _softcap(x):
    return GATE_SOFTCAP * jnp.tanh(x / GATE_SOFTCAP)


def _rms(x, g):
    return x * lax.rsqrt(jnp.mean(x * x, axis=-1, keepdims=True) + EPS) * g


def _lane_pick(x, lane_idx):
    lane = lax.broadcasted_iota(jnp.int32, x.shape, 1)
    return jnp.sum(jnp.where(lane == lane_idx, x, 0.0), axis=-1, keepdims=True)


def _tri_masks(n):
    r = lax.broadcasted_iota(jnp.int32, (n, n), 0)
    c = lax.broadcasted_iota(jnp.int32, (n, n), 1)
    return r >= c, r > c, r == c


def _cumsum_rows(col, causal_f32):
    L = col.shape[0]
    return _dot_hi(causal_f32, jnp.broadcast_to(col, (L, LANES)))


def _row_of(col_b):
    L = col_b.shape[0]
    lane = lax.broadcasted_iota(jnp.int32, (L, LANES), 1)
    e0 = (lane == 0).astype(F32)
    return _dot_nt_hi(e0, col_b)


def _transpose_rows(x):
    d = x.shape[1]
    _, _, eye = _tri_masks(d)
    return _dot_nt_hi(eye.astype(F32), x)


def _rmsnorm_kernel(x_ref, g_ref, o_ref):
    o_ref[...] = _rms(x_ref[...], g_ref[...]).astype(o_ref.dtype)


def _rmsnorm(x, g, out_dtype):
    m, d = x.shape
    tr = _largest_divisor(m, NORM_ROWS_CAP, BF16_ROWS)
    return pl.pallas_call(
        _rmsnorm_kernel,
        grid=(m // tr,),
        in_specs=[pl.BlockSpec((tr, d), lambda i: (i, 0)),
                  pl.BlockSpec((1, d), lambda i: (0, 0))],
        out_specs=pl.BlockSpec((tr, d), lambda i: (i, 0)),
        out_shape=jax.ShapeDtypeStruct((m, d), out_dtype),
        compiler_params=_cparams(("parallel",)),
        name="rmsnorm",
    )(x, g.reshape(1, d).astype(F32))


def _mm_epilogue(acc, mode, r_ref, o_ref):
    if mode == "residual":
        acc = acc + r_ref[...]
    elif mode == "relu2":
        acc = jnp.square(jnp.maximum(acc, 0.0))
    o_ref[...] = acc.astype(o_ref.dtype)


def _mm_kernel_single(*refs, mode):
    if mode == "residual":
        a_ref, w_ref, r_ref, o_ref = refs
    else:
        (a_ref, w_ref, o_ref), r_ref = refs, None
    acc = jnp.dot(a_ref[...], w_ref[...], preferred_element_type=F32)
    _mm_epilogue(acc, mode, r_ref, o_ref)


def _mm_kernel_acc(*refs, mode, nk):
    if mode == "residual":
        a_ref, w_ref, r_ref, o_ref, acc_ref = refs
    else:
        (a_ref, w_ref, o_ref, acc_ref), r_ref = refs, None
    k = pl.program_id(2)

    @pl.when(k == 0)
    def _():
        acc_ref[...] = jnp.zeros_like(acc_ref)

    acc_ref[...] += jnp.dot(a_ref[...], w_ref[...], preferred_element_type=F32)

    @pl.when(k == nk - 1)
    def _():
        _mm_epilogue(acc_ref[...], mode, r_ref, o_ref)


def _matmul(a, w, *, mode="plain", res=None, out_dtype=F32, tn_cap=TN_CAP, tk_cap=TK_CAP, name="matmul"):
    m, k = a.shape
    n = w.shape[1]
    tm = _largest_divisor(m, TM_CAP, BF16_ROWS)
    tn = _largest_divisor(n, tn_cap, LANES)
    tk = _largest_divisor(k, tk_cap, LANES)
    nk = k // tk
    args = [a, w]
    if nk == 1:
        in_specs = [pl.BlockSpec((tm, k), lambda i, j: (i, 0)),
                    pl.BlockSpec((k, tn), lambda i, j: (0, j))]
        io_spec = pl.BlockSpec((tm, tn), lambda i, j: (i, j))
        grid = (m // tm, n // tn)
        kern = functools.partial(_mm_kernel_single, mode=mode)
        scratch = []
        sem = ("parallel", "parallel")
    else:
        in_specs = [pl.BlockSpec((tm, tk), lambda i, j, kk: (i, kk)),
                    pl.BlockSpec((tk, tn), lambda i, j, kk: (kk, j))]
        io_spec = pl.BlockSpec((tm, tn), lambda i, j, kk: (i, j))
        grid = (m // tm, n // tn, nk)
        kern = functools.partial(_mm_kernel_acc, mode=mode, nk=nk)
        scratch = [pltpu.VMEM((tm, tn), F32)]
        sem = ("parallel", "parallel", "arbitrary")
    if mode == "residual":
        in_specs.append(io_spec)
        args.append(res)
    return pl.pallas_call(
        kern,
        grid=grid,
        in_specs=in_specs,
        out_specs=io_spec,
        out_shape=jax.ShapeDtypeStruct((m, n), out_dtype),
        scratch_shapes=scratch,
        compiler_params=_cparams(sem),
        name=name,
    )(*args)


def _merge_kernel(b0_ref, b1_ref, b2_ref, w_ref, g0_ref, g1_ref, g2_ref, o_ref):
    acc = None
    for n, (b_ref, g_ref) in enumerate(((b0_ref, g0_ref), (b1_ref, g1_ref), (b2_ref, g2_ref))):
        up = jnp.dot(b_ref[...], w_ref[n], preferred_element_type=F32)
        term = _sigmoid(g_ref[...]) * up
        acc = term if acc is None else acc + term
    o_ref[...] = acc.astype(o_ref.dtype)


def _merge(branches, w_br, proj, gate_off, d_model):
    m, bw = branches[0].shape
    tm = _largest_divisor(m, 768, BF16_ROWS)
    tn = _largest_divisor(math.gcd(d_model, gate_off), 512, LANES)
    g0 = gate_off // tn
    nj = d_model // tn
    br_spec = pl.BlockSpec((tm, bw), lambda i, j: (i, 0))

    def gate_spec(n):
        return pl.BlockSpec((tm, tn), lambda i, j: (i, g0 + n * nj + j))

    return pl.pallas_call(
        _merge_kernel,
        grid=(m // tm, nj),
        in_specs=[br_spec, br_spec, br_spec,
                  pl.BlockSpec((3, bw, tn), lambda i, j: (0, 0, j)),
                  gate_spec(0), gate_spec(1), gate_spec(2)],
        out_specs=pl.BlockSpec((tm, tn), lambda i, j: (i, j)),
        out_shape=jax.ShapeDtypeStruct((m, d_model), BF16),
        compiler_params=_cparams(("parallel", "parallel")),
        name="branch_merge",
    )(branches[0], branches[1], branches[2], w_br, proj, proj, proj)


def _pick_chunk(main_ref, meta_ref, is_meta, L, n_meta):
    meta = meta_ref[...]
    padded = jnp.concatenate([jnp.zeros((L - n_meta, meta.shape[1]), F32), meta], axis=0)
    return jnp.where(is_meta, padded, main_ref[...])


def _valid_rows(is_meta, L, n_meta):
    row = lax.broadcasted_iota(jnp.int32, (L, 1), 0)
    return jnp.logical_or(jnp.logical_not(is_meta), row >= L - n_meta)


def _unit_lower_inverse(nmat, eye_f32):
    L = nmat.shape[0]
    x = eye_f32 - nmat
    y = _dot_hi(nmat, nmat)
    p = 2
    while True:
        x = x + _dot_hi(x, y)
        p *= 2
        if p >= L:
            break
        y = _dot_hi(y, y)
    return x


def _gdn_prompt_kernel(qm_ref, km_ref, vm_ref, zm_ref, sm_ref,
                       qe_ref, ke_ref, ve_ref, ze_ref, se_ref,
                       cwq_ref, cwk_ref, cwv_ref, par_ref, ng_ref,
                       om_ref, oe_ref, st_ref,
                       s_scr, tail_scr, *, L, n_meta, n_heads, nc):
    h = pl.program_id(1)
    c = pl.program_id(2)
    is_meta = c == 0
    dk = qm_ref.shape[1]

    @pl.when(is_meta)
    def _():
        s_scr[...] = jnp.zeros_like(s_scr)
        tail_scr[...] = jnp.zeros_like(tail_scr)

    valid = _valid_rows(is_meta, L, n_meta)
    x3 = jnp.concatenate([_pick_chunk(qm_ref, qe_ref, is_meta, L, n_meta),
                          _pick_chunk(km_ref, ke_ref, is_meta, L, n_meta),
                          _pick_chunk(vm_ref, ve_ref, is_meta, L, n_meta)], axis=1)
    z = _pick_chunk(zm_ref, ze_ref, is_meta, L, n_meta)
    sm = _pick_chunk(sm_ref, se_ref, is_meta, L, n_meta)

    w3 = jnp.concatenate([cwq_ref[...], cwk_ref[...], cwv_ref[...]], axis=1)
    kw = w3.shape[0]
    xp = jnp.concatenate([tail_scr[...], x3], axis=0)
    y = w3[kw - 1:kw] * x3
    for j in range(1, kw):
        y = y + w3[kw - 1 - j:kw - j] * pltpu.roll(xp, j, axis=0)[SUBLANES:SUBLANES + L]
    tail_scr[...] = x3[L - SUBLANES:L]
    y = _silu(y)
    q = y[:, :dk]
    k = y[:, dk:2 * dk]
    v = y[:, 2 * dk:]
    q = q * lax.rsqrt(jnp.sum(q * q, axis=-1, keepdims=True) + EPS) * dk ** -0.5
    k = k * lax.rsqrt(jnp.sum(k * k, axis=-1, keepdims=True) + EPS)

    g_all = -jnp.exp(par_ref[0:1, :]) * _softplus(sm + par_ref[1:2, :])
    g = jnp.where(valid, _lane_pick(g_all, h), 0.0)
    beta = jnp.where(valid, _lane_pick(_sigmoid(sm), n_heads + h), 0.0)

    causal, strict, eye = _tri_masks(L)
    causal_f = causal.astype(F32)
    eye_f = eye.astype(F32)
    s = s_scr[...]

    gc = _cumsum_rows(g, causal_f)
    diff = gc[:, :L] - _row_of(gc)
    decay = jnp.where(causal, jnp.exp(jnp.where(causal, diff, 0.0)), 0.0)
    eg = jnp.exp(gc)
    kb = k * beta
    nmat = jnp.where(strict, _dot_nt(kb, k) * decay, 0.0)
    tinv = _unit_lower_inverse(nmat, eye_f)
    rhs = jnp.concatenate([v * beta, kb * eg], axis=1)
    sol = _dot_hi(tinv, rhs)
    u = sol[:, :dk]
    w = sol[:, dk:]
    v_new = u - _dot(w, s)
    attn = _dot_nt(q, k) * decay
    o = _dot(q * eg, s) + _dot(attn, v_new)
    g_last = gc[L - 1:L, :]
    s_new = s * jnp.exp(g_last) + _dot_tn(k * jnp.exp(g_last - gc), v_new)
    s_scr[...] = s_new

    out = (_rms(o, ng_ref[...]) * _silu(z)).astype(om_ref.dtype)

    @pl.when(is_meta)
    def _():
        oe_ref[...] = out[L - n_meta:]

    @pl.when(jnp.logical_not(is_meta))
    def _():
        om_ref[...] = out

    @pl.when(c == nc)
    def _():
        st_ref[0, 0] = s_new


def _prompt_specs(L, n_meta, width, col_blk, nc, meta_blk0):
    main = pl.BlockSpec((L, width), lambda b, h, c: (b * nc + jnp.maximum(c - 1, 0), col_blk + h))
    meta = pl.BlockSpec((n_meta, width), lambda b, h, c: (meta_blk0 + b, col_blk + h))
    return main, meta


def _small_specs(L, n_meta, nc, meta_blk0):
    main = pl.BlockSpec((L, LANES), lambda b, h, c: (b * nc + jnp.maximum(c - 1, 0), 0))
    meta = pl.BlockSpec((n_meta, LANES), lambda b, h, c: (meta_blk0 + b, 0))
    return main, meta


def _out_specs(L, n_meta, width, nc):
    main = pl.BlockSpec((L, width), lambda b, h, c: (b * nc + jnp.maximum(c - 1, 0), h))
    meta = pl.BlockSpec((n_meta, width), lambda b, h, c: (b, h))
    return main, meta


def _gdn_prompt(proj, small, conv_w, par, ng, lay):
    L, n_meta, B, T, H = CHUNK, lay["n_meta"], lay["B"], lay["T"], lay["gdn_heads"]
    dk = lay["gdn_dk"]
    nc = T // L
    meta_blk0 = lay["meta_row0"] // n_meta
    seg = lay["seg"]
    specs_main, specs_meta = [], []
    for off in (seg["qkv"], seg["qkv"] + H * dk, seg["qkv"] + 2 * H * dk, seg["z"]):
        a, b = _prompt_specs(L, n_meta, dk, off // dk, nc, meta_blk0)
        specs_main.append(a)
        specs_meta.append(b)
    sm_main, sm_meta = _small_specs(L, n_meta, nc, meta_blk0)
    kw = conv_w.shape[0]
    cw_specs = [pl.BlockSpec((kw, dk), functools.partial(lambda b, h, c, o: (0, o + h), o=o))
                for o in (0, H, 2 * H)]
    full2 = lambda b, h, c: (0, 0)
    out_main, out_meta = _out_specs(L, n_meta, dk, nc)
    kern = functools.partial(_gdn_prompt_kernel, L=L, n_meta=n_meta, n_heads=H, nc=nc)
    return pl.pallas_call(
        kern,
        grid=(B, H, nc + 1),
        in_specs=specs_main + [sm_main] + specs_meta + [sm_meta] + cw_specs
        + [pl.BlockSpec(par.shape, full2), pl.BlockSpec((1, dk), full2)],
        out_specs=[out_main, out_meta,
                   pl.BlockSpec((1, 1, dk, dk), lambda b, h, c: (b, h, 0, 0))],
        out_shape=[jax.ShapeDtypeStruct((B * T, H * dk), BF16),
                   jax.ShapeDtypeStruct((B * n_meta, H * dk), BF16),
                   jax.ShapeDtypeStruct((B, H, dk, dk), F32)],
        scratch_shapes=[pltpu.VMEM((dk, dk), F32), pltpu.VMEM((SUBLANES, 3 * dk), F32)],
        compiler_params=_cparams(("parallel", "parallel", "arbitrary")),
        name="gdn_prompt",
    )(proj, proj, proj, proj, small, proj, proj, proj, proj, small,
      conv_w, conv_w, conv_w, par, ng.reshape(1, dk))


def _hgrn_level_matrix(L):
    mats = [np.tril(np.ones((L, L), np.float32))]
    bs = L
    while bs >= 2:
        half = bs // 2
        m = np.zeros((L, L), np.float32)
        for t in range(L):
            mid = (t // bs) * bs + half
            if t % bs >= half:
                m[t, mid:t + 1] = 1.0
            else:
                m[t, t + 1:mid] = 1.0
        mats.append(m)
        bs //= 2
    return np.concatenate(mats, axis=0)


def _hgrn_prompt_kernel(qm_ref, fm_ref, im_ref, gm_ref,
                        qe_ref, fe_ref, ie_ref, ge_ref,
                        lb_ref, lvl_ref, ng_ref,
                        om_ref, oe_ref, st_ref,
                        st_scr, *, L, n_meta, nc):
    c = pl.program_id(2)
    is_meta = c == 0

    @pl.when(is_meta)
    def _():
        st_scr[...] = jnp.zeros_like(st_scr)

    valid = _valid_rows(is_meta, L, n_meta)
    q = _pick_chunk(qm_ref, qe_ref, is_meta, L, n_meta)
    hf = _pick_chunk(fm_ref, fe_ref, is_meta, L, n_meta)
    v = _pick_chunk(im_ref, ie_ref, is_meta, L, n_meta)
    gate = _pick_chunk(gm_ref, ge_ref, is_meta, L, n_meta)
    lb = lb_ref[...]
    f = lb + (1.0 - lb) * _sigmoid(hf)
    lf = jnp.where(valid, jnp.log(jnp.maximum(f, TINY)), 0.0)
    k = jnp.where(valid, (1.0 - lb) * _sigmoid(-hf), 0.0)

    e = _dot_hi(lvl_ref[...], lf)
    bcum = e[:L]
    row = lax.broadcasted_iota(jnp.int32, (L, 1), 0)
    r2 = lax.broadcasted_iota(jnp.int32, (L, L), 0)
    c2 = lax.broadcasted_iota(jnp.int32, (L, L), 1)
    attn = jnp.where(r2 == c2, jnp.sum(q * k, axis=-1, keepdims=True), 0.0)
    bs = L
    lev = 1
    while bs >= 2:
        half = bs // 2
        x = jnp.exp(e[lev * L:(lev + 1) * L])
        second = (row % bs) >= half
        qt = jnp.where(second, q * x, 0.0)
        kt = jnp.where(second, 0.0, k * x)
        pair = jnp.logical_and(jnp.logical_and((r2 // bs) == (c2 // bs), (r2 % bs) >= half),
                               (c2 % bs) < half)
        attn = attn + jnp.where(pair, _dot_nt(qt, kt), 0.0)
        bs //= 2
        lev += 1

    st = st_scr[...]
    o = _dot_nt(q * jnp.exp(bcum), st) + _dot(attn, v)
    b_last = bcum[L - 1:L, :]
    st_new = st * jnp.exp(b_last) + _dot_tn(v, k * jnp.exp(b_last - bcum))
    st_scr[...] = st_new

    out = (_rms(o, ng_ref[...]) * _silu(gate)).astype(om_ref.dtype)

    @pl.when(is_meta)
    def _():
        oe_ref[...] = out[L - n_meta:]

    @pl.when(jnp.logical_not(is_meta))
    def _():
        om_ref[...] = out

    @pl.when(c == nc)
    def _():
        st_ref[0, 0] = st_new.T


def _hgrn_prompt(proj, lb, ng, lay):
    L, n_meta, B, T, H = CHUNK, lay["n_meta"], lay["B"], lay["T"], lay["hg_heads"]
    dk = lay["hg_dk"]
    nc = T // L
    meta_blk0 = lay["meta_row0"] // n_meta
    seg = lay["seg"]
    specs_main, specs_meta = [], []
    for name in ("hq", "hf", "hi", "hgate"):
        a, b = _prompt_specs(L, n_meta, dk, seg[name] // dk, nc, meta_blk0)
        specs_main.append(a)
        specs_meta.append(b)
    lvl = jnp.asarray(_hgrn_level_matrix(L))
    full2 = lambda b, h, c: (0, 0)
    out_main, out_meta = _out_specs(L, n_meta, dk, nc)
    kern = functools.partial(_hgrn_prompt_kernel, L=L, n_meta=n_meta, nc=nc)
    return pl.pallas_call(
        kern,
        grid=(B, H, nc + 1),
        in_specs=specs_main + specs_meta
        + [pl.BlockSpec((1, dk), lambda b, h, c: (0, h)),
           pl.BlockSpec(lvl.shape, full2), pl.BlockSpec((1, dk), full2)],
        out_specs=[out_main, out_meta,
                   pl.BlockSpec((1, 1, dk, dk), lambda b, h, c: (b, h, 0, 0))],
        out_shape=[jax.ShapeDtypeStruct((B * T, H * dk), BF16),
                   jax.ShapeDtypeStruct((B * n_meta, H * dk), BF16),
                   jax.ShapeDtypeStruct((B, H, dk, dk), F32)],
        scratch_shapes=[pltpu.VMEM((dk, dk), F32)],
        compiler_params=_cparams(("parallel", "parallel", "arbitrary")),
        name="hgrn_prompt",
    )(proj, proj, proj, proj, proj, proj, proj, proj, lb.reshape(1, H * dk), lvl, ng.reshape(1, dk))


def _mlstm_prompt_kernel(qm_ref, km_ref, vm_ref, om_in_ref, sm_ref,
                         qe_ref, ke_ref, ve_ref, oe_in_ref, se_ref,
                         par_ref, ng_ref,
                         om_ref, oe_ref, c_ref, n_ref, m_ref,
                         c_scr, n_scr, m_scr, *, L, n_meta, li_lane0, lf_lane0, nc):
    h = pl.program_id(1)
    c = pl.program_id(2)
    is_meta = c == 0
    dk = qm_ref.shape[1]

    @pl.when(is_meta)
    def _():
        c_scr[...] = jnp.zeros_like(c_scr)
        n_scr[...] = jnp.zeros_like(n_scr)
        m_scr[...] = jnp.zeros_like(m_scr)

    valid = _valid_rows(is_meta, L, n_meta)
    q = _pick_chunk(qm_ref, qe_ref, is_meta, L, n_meta) * dk ** -0.5
    k = _pick_chunk(km_ref, ke_ref, is_meta, L, n_meta)
    v = _pick_chunk(vm_ref, ve_ref, is_meta, L, n_meta)
    og = _pick_chunk(om_in_ref, oe_in_ref, is_meta, L, n_meta)
    sm = _pick_chunk(sm_ref, se_ref, is_meta, L, n_meta)

    li = jnp.where(valid, _lane_pick(_softcap(sm + par_ref[2:3, :]), li_lane0 + h), NEG_BIG)
    lf = jnp.where(valid, _lane_pick(_log_sigmoid(_softcap(sm + par_ref[3:4, :])), lf_lane0 + h), 0.0)

    causal, _, _ = _tri_masks(L)
    cmat = c_scr[...]
    nrow = n_scr[...]
    m_prev = m_scr[:, :1]

    bb = _cumsum_rows(lf, causal.astype(F32))
    b = bb[:, :1]
    d = bb[:, :L] - _row_of(bb - li)
    inter = b + m_prev
    m_t = jnp.maximum(inter, jnp.max(jnp.where(causal, d, NEG_BIG), axis=-1, keepdims=True))
    dw = jnp.where(causal, jnp.exp(jnp.where(causal, d - m_t, 0.0)), 0.0)
    iw = jnp.exp(inter - m_t)
    qk = _dot_nt(q, k) * dw
    num = iw * _dot(q, cmat) + _dot(qk, v)
    den = iw * jnp.sum(q * nrow, axis=-1, keepdims=True) + jnp.sum(qk, axis=-1, keepdims=True)
    hh = num / jnp.maximum(jnp.abs(den), jnp.exp(-m_t))

    b_last = b[L - 1:L, :]
    d_last = b_last - b + li
    m_new = jnp.maximum(b_last + m_prev, jnp.max(d_last, axis=0, keepdims=True))
    w_last = jnp.exp(d_last - m_new)
    dec = jnp.exp(b_last + m_prev - m_new)
    kw = k * w_last
    c_new = dec * cmat + _dot_tn(kw, v)
    n_new = dec * nrow + jnp.sum(kw, axis=0, keepdims=True)
    m_new_b = jnp.broadcast_to(m_new, m_scr.shape)
    c_scr[...] = c_new
    n_scr[...] = n_new
    m_scr[...] = m_new_b

    out = (_rms(hh, ng_ref[...]) * _sigmoid(og)).astype(om_ref.dtype)

    @pl.when(is_meta)
    def _():
        oe_ref[...] = out[L - n_meta:]

    @pl.when(jnp.logical_not(is_meta))
    def _():
        om_ref[...] = out

    @pl.when(c == nc)
    def _():
        c_ref[0, 0] = c_new
        n_ref[0, 0] = n_new
        m_ref[0, 0] = m_new_b


def _mlstm_prompt(proj, small, par, ng, lay):
    L, n_meta, B, T, H = CHUNK, lay["n_meta"], lay["B"], lay["T"], lay["ml_heads"]
    dk, dv = lay["ml_dk"], lay["ml_dv"]
    nc = T // L
    meta_blk0 = lay["meta_row0"] // n_meta
    seg = lay["seg"]
    specs_main, specs_meta = [], []
    for name, width in (("mq", dk), ("mk", dk), ("mv", dv), ("mo", dv)):
        assert seg[name] % width == 0
        a, b = _prompt_specs(L, n_meta, width, seg[name] // width, nc, meta_blk0)
        specs_main.append(a)
        specs_meta.append(b)
    sm_main, sm_meta = _small_specs(L, n_meta, nc, meta_blk0)
    full2 = lambda b, h, c: (0, 0)
    out_main, out_meta = _out_specs(L, n_meta, dv, nc)
    kern = functools.partial(_mlstm_prompt_kernel, L=L, n_meta=n_meta, nc=nc,
                             li_lane0=lay["li_lane0"], lf_lane0=lay["lf_lane0"])
    st_idx = lambda b, h, c: (b, h, 0, 0)
    return pl.pallas_call(
        kern,
        grid=(B, H, nc + 1),
        in_specs=specs_main + [sm_main] + specs_meta + [sm_meta]
        + [pl.BlockSpec(par.shape, full2), pl.BlockSpec((1, dv), full2)],
        out_specs=[out_main, out_meta,
                   pl.BlockSpec((1, 1, dk, dv), st_idx),
                   pl.BlockSpec((1, 1, 1, dk), st_idx),
                   pl.BlockSpec((1, 1, 1, LANES), st_idx)],
        out_shape=[jax.ShapeDtypeStruct((B * T, H * dv), BF16),
                   jax.ShapeDtypeStruct((B * n_meta, H * dv), BF16),
                   jax.ShapeDtypeStruct((B, H, dk, dv), F32),
                   jax.ShapeDtypeStruct((B, H, 1, dk), F32),
                   jax.ShapeDtypeStruct((B, H, 1, LANES), F32)],
        scratch_shapes=[pltpu.VMEM((dk, dv), F32), pltpu.VMEM((1, dk), F32), pltpu.VMEM((1, LANES), F32)],
        compiler_params=_cparams(("parallel", "parallel", "arbitrary")),
        name="mlstm_prompt",
    )(proj, proj, proj, proj, small, proj, proj, proj, proj, small, par, ng.reshape(1, dv))


def _gdn_sample_kernel(xq_ref, xk_ref, xv_ref, z_ref, sm_ref,
                       cq0, cq1, cq2, ck0, ck1, ck2, cv0, cv1, cv2,
                       cwq_ref, cwk_ref, cwv_ref, par_ref, ng_ref, s_ref,
                       o_ref, so_ref, *, n_heads, rows):
    h = pl.program_id(0)
    dk = xq_ref.shape[1]

    def conv(x_ref, taps, cw_ref):
        w = cw_ref[...]
        kw = w.shape[0]
        y = w[kw - 1:kw] * x_ref[...]
        for j, t_ref in enumerate(taps):
            y = y + w[j:j + 1] * t_ref[...]
        return _silu(y)

    q = conv(xq_ref, (cq0, cq1, cq2), cwq_ref)
    k = conv(xk_ref, (ck0, ck1, ck2), cwk_ref)
    v = conv(xv_ref, (cv0, cv1, cv2), cwv_ref)
    q = q * lax.rsqrt(jnp.sum(q * q, axis=-1, keepdims=True) + EPS) * dk ** -0.5
    k = k * lax.rsqrt(jnp.sum(k * k, axis=-1, keepdims=True) + EPS)
    sm = sm_ref[...]
    g = _lane_pick(-jnp.exp(par_ref[0:1, :]) * _softplus(sm + par_ref[1:2, :]), h)
    beta = _lane_pick(_sigmoid(sm), n_heads + h)
    eg = jnp.exp(g)
    q_t = _transpose_rows(q)
    k_t = _transpose_rows(k)
    outs = []
    for j in range(rows):
        s = s_ref[j]
        kc = k_t[:, j:j + 1]
        qc = q_t[:, j:j + 1]
        egj = eg[j:j + 1, :]
        ks = jnp.sum(kc * s, axis=0, keepdims=True)
        v_new = beta[j:j + 1, :] * (v[j:j + 1, :] - egj * ks)
        s_new = s * egj + kc * v_new
        so_ref[j] = s_new
        outs.append(jnp.sum(qc * s_new, axis=0, keepdims=True))
    o = jnp.concatenate(outs, axis=0)
    o_ref[...] = (_rms(o, ng_ref[...]) * _silu(z_ref[...])).astype(o_ref.dtype)


def _gdn_sample(proj, small, conv_state, state, layer, conv_w, par, ng, lay):
    S, H, dk = lay["S"], lay["gdn_heads"], lay["gdn_dk"]
    rows = SAMPLE_ROWS
    assert S % rows == 0 and lay["sample_row0"] % rows == 0
    r0 = lay["sample_row0"] // rows
    seg = lay["seg"]
    kw = conv_w.shape[0]
    ch = 3 * H * dk
    cs = conv_state.reshape(conv_state.shape[0], S, (kw - 1) * ch)
    col = {"q": seg["qkv"] // dk, "k": seg["qkv"] // dk + H, "v": seg["qkv"] // dk + 2 * H}
    x_specs = [pl.BlockSpec((rows, dk), functools.partial(lambda h, i, o: (r0 + i, o + h), o=o))
               for o in (col["q"], col["k"], col["v"], seg["z"] // dk)]
    sm_spec = pl.BlockSpec((rows, LANES), lambda h, i: (r0 + i, 0))
    tap_specs = [pl.BlockSpec((None, rows, dk),
                              functools.partial(lambda h, i, o: (layer, i, o + h), o=j * (ch // dk) + o))
                 for o in (0, H, 2 * H) for j in range(kw - 1)]
    cw_specs = [pl.BlockSpec((kw, dk), functools.partial(lambda h, i, o: (0, o + h), o=o))
                for o in (0, H, 2 * H)]
    full2 = lambda h, i: (0, 0)
    st_spec = pl.BlockSpec((None, rows, None, dk, dk), lambda h, i: (layer, i, h, 0, 0))
    kern = functools.partial(_gdn_sample_kernel, n_heads=H, rows=rows)
    return pl.pallas_call(
        kern,
        grid=(H, S // rows),
        in_specs=x_specs + [sm_spec] + tap_specs + cw_specs
        + [pl.BlockSpec(par.shape, full2), pl.BlockSpec((1, dk), full2), st_spec],
        out_specs=[pl.BlockSpec((rows, dk), lambda h, i: (i, h)),
                   pl.BlockSpec((rows, None, dk, dk), lambda h, i: (i, h, 0, 0))],
        out_shape=[jax.ShapeDtypeStruct((S, H * dk), BF16),
                   jax.ShapeDtypeStruct((S, H, dk, dk), F32)],
        compiler_params=_cparams(("parallel", "parallel")),
        name="gdn_sample",
    )(proj, proj, proj, proj, small, *([cs] * (3 * (kw - 1))), conv_w, conv_w, conv_w,
      par, ng.reshape(1, dk), state)


def _hgrn_sample_kernel(q_ref, f_ref, i_ref, g_ref, lb_ref, ng_ref, s_ref, o_ref, so_ref, *, rows):
    lb = lb_ref[...]
    hf = f_ref[...]
    f = jnp.maximum(lb + (1.0 - lb) * _sigmoid(hf), TINY)
    k = (1.0 - lb) * _sigmoid(-hf)
    v = i_ref[...]
    q_t = _transpose_rows(q_ref[...])
    k_t = _transpose_rows(k)
    f_t = _transpose_rows(f)
    outs = []
    for j in range(rows):
        s_new = f_t[:, j:j + 1] * s_ref[j] + k_t[:, j:j + 1] * v[j:j + 1, :]
        so_ref[j] = s_new
        outs.append(jnp.sum(q_t[:, j:j + 1] * s_new, axis=0, keepdims=True))
    o = jnp.concatenate(outs, axis=0)
    o_ref[...] = (_rms(o, ng_ref[...]) * _silu(g_ref[...])).astype(o_ref.dtype)


def _hgrn_sample(proj, state, layer, lb, ng, lay):
    S, H, dk = lay["S"], lay["hg_heads"], lay["hg_dk"]
    rows = SAMPLE_ROWS
    r0 = lay["sample_row0"] // rows
    seg = lay["seg"]
    x_specs = [pl.BlockSpec((rows, dk), functools.partial(lambda h, i, o: (r0 + i, o + h), o=seg[n] // dk))
               for n in ("hq", "hf", "hi", "hgate")]
    st_spec = pl.BlockSpec((None, rows, None, dk, dk), lambda h, i: (layer, i, h, 0, 0))
    return pl.pallas_call(
        functools.partial(_hgrn_sample_kernel, rows=rows),
        grid=(H, S // rows),
        in_specs=x_specs + [pl.BlockSpec((1, dk), lambda h, i: (0, h)),
                            pl.BlockSpec((1, dk), lambda h, i: (0, 0)), st_spec],
        out_specs=[pl.BlockSpec((rows, dk), lambda h, i: (i, h)),
                   pl.BlockSpec((rows, None, dk, dk), lambda h, i: (i, h, 0, 0))],
        out_shape=[jax.ShapeDtypeStruct((S, H * dk), BF16),
                   jax.ShapeDtypeStruct((S, H, dk, dk), F32)],
        compiler_params=_cparams(("parallel", "parallel")),
        name="hgrn_sample",
    )(proj, proj, proj, proj, lb.reshape(1, H * dk), ng.reshape(1, dk), state)


def _mlstm_sample_kernel(q_ref, k_ref, v_ref, og_ref, sm_ref, par_ref, ng_ref,
                         c_ref, n_ref, m_ref,
                         o_ref, co_ref, no_ref, mo_ref, *, rows, li_lane0, lf_lane0):
    h = pl.program_id(0)
    dk = q_ref.shape[1]
    q = q_ref[...] * dk ** -0.5
    k = k_ref[...]
    v = v_ref[...]
    sm = sm_ref[...]
    li = _lane_pick(_softcap(sm + par_ref[2:3, :]), li_lane0 + h)
    lf = _lane_pick(_log_sigmoid(_softcap(sm + par_ref[3:4, :])), lf_lane0 + h)
    m_prev = m_ref[...]
    m_new = jnp.maximum(lf + m_prev, li)
    dw = jnp.exp(li - m_new)
    dec = jnp.exp(lf + m_prev - m_new)
    n_new = dec * n_ref[...] + k * dw
    den = jnp.sum(q * n_new, axis=-1, keepdims=True)
    scale = 1.0 / jnp.maximum(jnp.abs(den), jnp.exp(-m_new))
    no_ref[...] = n_new
    mo_ref[...] = m_new
    q_t = _transpose_rows(q)
    k_t = _transpose_rows(k * dw)
    outs = []
    for j in range(rows):
        c_new = dec[j:j + 1, :] * c_ref[j] + k_t[:, j:j + 1] * v[j:j + 1, :]
        co_ref[j] = c_new
        outs.append(jnp.sum(q_t[:, j:j + 1] * c_new, axis=0, keepdims=True))
    hh = jnp.concatenate(outs, axis=0) * scale
    o_ref[...] = (_rms(hh, ng_ref[...]) * _sigmoid(og_ref[...])).astype(o_ref.dtype)


def _mlstm_sample(proj, small, c_state, n_state_t, m_state_t, layer, par, ng, lay):
    S, H, dk, dv = lay["S"], lay["ml_heads"], lay["ml_dk"], lay["ml_dv"]
    rows = ML_SAMPLE_ROWS
    assert S % rows == 0 and lay["sample_row0"] % rows == 0
    r0 = lay["sample_row0"] // rows
    seg = lay["seg"]
    x_specs = [pl.BlockSpec((rows, w), functools.partial(lambda h, i, o: (r0 + i, o + h), o=seg[n] // w))
               for n, w in (("mq", dk), ("mk", dk), ("mv", dv), ("mo", dv))]
    full2 = lambda h, i: (0, 0)
    kern = functools.partial(_mlstm_sample_kernel, rows=rows,
                             li_lane0=lay["li_lane0"], lf_lane0=lay["lf_lane0"])
    return pl.pallas_call(
        kern,
        grid=(H, S // rows),
        in_specs=x_specs + [pl.BlockSpec((rows, LANES), lambda h, i: (r0 + i, 0)),
                            pl.BlockSpec(par.shape, full2), pl.BlockSpec((1, dv), full2),
                            pl.BlockSpec((None, rows, None, dk, dv), lambda h, i: (layer, i, h, 0, 0)),
                            pl.BlockSpec((None, None, rows, dk), lambda h, i: (layer, h, i, 0)),
                            pl.BlockSpec((None, None, rows, 1), lambda h, i: (layer, h, i, 0))],
        out_specs=[pl.BlockSpec((rows, dv), lambda h, i: (i, h)),
                   pl.BlockSpec((rows, None, dk, dv), lambda h, i: (i, h, 0, 0)),
                   pl.BlockSpec((None, rows, dk), lambda h, i: (h, i, 0)),
                   pl.BlockSpec((None, rows, 1), lambda h, i: (h, i, 0))],
        out_shape=[jax.ShapeDtypeStruct((S, H * dv), F32),
                   jax.ShapeDtypeStruct((S, H, dk, dv), F32),
                   jax.ShapeDtypeStruct((H, S, dk), F32),
                   jax.ShapeDtypeStruct((H, S, 1), F32)],
        compiler_params=_cparams(("parallel", "parallel")),
        name="mlstm_sample",
    )(proj, proj, proj, proj, small, par, ng.reshape(1, dv), c_state, n_state_t, m_state_t)


def _layout(x_prompt, x_sample, meta_tokens, state_gdn_s, state_hg_s, state_ml_c):
    B, T, D = x_prompt.shape
    S = x_sample.shape[0] * x_sample.shape[1]
    n_meta = meta_tokens.shape[0]
    gdn_heads, gdn_dk = state_gdn_s.shape[2], state_gdn_s.shape[3]
    hg_heads, hg_dk = state_hg_s.shape[2], state_hg_s.shape[3]
    ml_heads, ml_dk, ml_dv = state_ml_c.shape[2], state_ml_c.shape[3], state_ml_c.shape[4]
    bw = gdn_heads * gdn_dk
    assert state_gdn_s.shape[4] == gdn_dk and state_hg_s.shape[4] == hg_dk
    assert hg_heads * hg_dk == bw and ml_heads * ml_dv == bw
    assert gdn_dk % LANES == 0 and hg_dk % LANES == 0 and ml_dk % LANES == 0
    assert T % CHUNK == 0 and n_meta % SUBLANES == 0 and n_meta <= CHUNK
    sizes = (("qkv", 3 * bw), ("z", bw), ("hq", bw), ("hf", bw), ("hi", bw), ("hgate", bw),
             ("mq", ml_heads * ml_dk), ("mk", ml_heads * ml_dk), ("mv", bw), ("mo", bw),
             ("gates", 3 * D))
    seg, acc = {}, 0
    for name, size in sizes:
        seg[name] = acc
        acc += size
    m = B * T + S + B * n_meta
    assert 2 * gdn_heads + 2 * ml_heads <= LANES
    assert (B * T) % SAMPLE_ROWS == 0 and (B * T + S) % n_meta == 0
    return dict(B=B, T=T, D=D, S=S, n_meta=n_meta, bw=bw, seg=seg, n_main=acc,
                gdn_heads=gdn_heads, gdn_dk=gdn_dk, hg_heads=hg_heads, hg_dk=hg_dk,
                ml_heads=ml_heads, ml_dk=ml_dk, ml_dv=ml_dv,
                rows=m, rows_padded=_round_up(m, ROW_ALIGN),
                sample_row0=B * T, meta_row0=B * T + S,
                li_lane0=2 * gdn_heads, lf_lane0=2 * gdn_heads + ml_heads)


def _split_w_in(w_in_l, lay):
    bw, gh, mh = lay["bw"], lay["gdn_heads"], lay["ml_heads"]
    mlk = mh * lay["ml_dk"]
    o_a = 4 * bw
    o_b = o_a + gh
    o_h = o_b + gh
    o_m = o_h + 4 * bw
    o_mi = o_m + 2 * mlk + 2 * bw
    o_mf = o_mi + mh
    o_g = o_mf + mh
    main = jnp.concatenate([w_in_l[:, :o_a], w_in_l[:, o_h:o_mi], w_in_l[:, o_g:]], axis=1)
    small = jnp.concatenate([w_in_l[:, o_a:o_h], w_in_l[:, o_mi:o_g]], axis=1)
    small = jnp.pad(small, ((0, 0), (0, LANES - small.shape[1])))
    return main.astype(BF16), small.astype(BF16)


def _gate_params(a_log, dt_bias, i_bias, f_bias, lay):
    gh, mh = lay["gdn_heads"], lay["ml_heads"]
    par = jnp.zeros((SUBLANES, LANES), F32)
    par = par.at[0, :gh].set(a_log.astype(F32))
    par = par.at[1, :gh].set(dt_bias.astype(F32))
    par = par.at[2, lay["li_lane0"]:lay["li_lane0"] + mh].set(i_bias.astype(F32))
    par = par.at[3, lay["lf_lane0"]:lay["lf_lane0"] + mh].set(f_bias.astype(F32))
    return par


def kernel(x_prompt, x_sample, state_gdn_s, state_gdn_conv, state_hg_s, state_ml_c, state_ml_n, state_ml_m, meta_tokens, norm1_g, w_in, gdn_conv_w, gdn_a_log, gdn_dt_bias, gdn_norm_g, hg_lb_logits, hg_norm_g, ml_i_bias, ml_f_bias, ml_norm_g, w_branch, w_out, norm2_g, w_up, w_down, final_norm_g):
    lay = _layout(x_prompt, x_sample, meta_tokens, state_gdn_s, state_hg_s, state_ml_c)
    B, T, D, S, n_meta = lay["B"], lay["T"], lay["D"], lay["S"], lay["n_meta"]
    depth = w_in.shape[0]
    kw = gdn_conv_w.shape[1]
    mp = lay["rows_padded"]
    r_s, r_m = lay["sample_row0"], lay["meta_row0"]

    h = jnp.concatenate([x_prompt.reshape(B * T, D), x_sample.reshape(S, D),
                         jnp.tile(meta_tokens.astype(x_prompt.dtype), (B, 1)),
                         jnp.zeros((mp - lay["rows"], D), x_prompt.dtype)], axis=0)

    lb_w = jax.nn.softmax(hg_lb_logits.astype(F32), axis=0)
    lb_all = jnp.cumsum(lb_w, axis=0) - lb_w[0]
    n_state_t = jnp.swapaxes(state_ml_n, 1, 2)
    m_state_t = jnp.swapaxes(state_ml_m, 1, 2)[..., None]
    pad_rows = jnp.zeros((mp - lay["rows"], lay["bw"]), BF16)

    new_p, new_s = [], []
    for l in range(depth):
        w_main, w_small = _split_w_in(w_in[l], lay)
        par = _gate_params(gdn_a_log[l], gdn_dt_bias[l], ml_i_bias[l], ml_f_bias[l], lay)

        xn = _rmsnorm(h, norm1_g[l], BF16)
        proj = _matmul(xn, w_main, name="in_proj")
        small = _matmul(xn, w_small, name="in_proj_gates")

        g_main, g_meta, p_gs = _gdn_prompt(proj, small, gdn_conv_w[l], par, gdn_norm_g[l], lay)
        h_main, h_meta, p_hs = _hgrn_prompt(proj, lb_all[l], hg_norm_g[l], lay)
        m_main, m_meta, p_c, p_n, p_m = _mlstm_prompt(proj, small, par, ml_norm_g[l], lay)

        g_smp, s_gs = _gdn_sample(proj, small, state_gdn_conv, state_gdn_s, l, gdn_conv_w[l], par,
                                  gdn_norm_g[l], lay)
        h_smp, s_hs = _hgrn_sample(proj, state_hg_s, l, lb_all[l], hg_norm_g[l], lay)
        m_smp, s_c, s_n, s_m = _mlstm_sample(proj, small, state_ml_c, n_state_t, m_state_t, l, par,
                                             ml_norm_g[l], lay)

        branches = [jnp.concatenate([a, b.astype(BF16), c, pad_rows], axis=0)
                    for a, b, c in ((g_main, g_smp, g_meta), (h_main, h_smp, h_meta), (m_main, m_smp, m_meta))]
        merged = _merge(branches, w_branch[l].astype(BF16), proj, lay["seg"]["gates"], D)
        h = _matmul(merged, w_out[l].astype(BF16), mode="residual", res=h, tn_cap=512, name="out_proj")

        xn2 = _rmsnorm(h, norm2_g[l], BF16)
        up = _matmul(xn2, w_up[l].astype(BF16), mode="relu2", out_dtype=BF16, name="mlp_up")
        h = _matmul(up, w_down[l].astype(BF16), mode="residual", res=h, tk_cap=2048, name="mlp_down")

        ch = 3 * lay["bw"]
        pre = proj[:, :ch]
        p_conv = pre[:B * T].reshape(B, T, ch)[:, T - (kw - 1):]
        s_conv = jnp.concatenate([state_gdn_conv[l][:, 1:], pre[r_s:r_s + S][:, None, :]], axis=1)
        new_p.append((p_gs, p_conv, p_hs, p_c, p_n[:, :, 0, :], p_m[:, :, 0, 0]))
        new_s.append((s_gs, s_conv, s_hs, s_c, jnp.swapaxes(s_n, 0, 1), jnp.swapaxes(s_m[..., 0], 0, 1)))

    y = _rmsnorm(h, final_norm_g, x_prompt.dtype)
    y_prompt = y[:B * T].reshape(B, T, D)
    y_sample = y[r_s:r_s + S].reshape(x_sample.shape)
    stack = lambda per_layer: tuple(jnp.stack([st[i] for st in per_layer], axis=0) for i in range(6))
    return (y_prompt, y_sample) + stack(new_p) + stack(new_s)
```

```python
import functools
import math

import numpy as np
import jax
import jax.numpy as jnp
from jax import lax
from jax.experimental import pallas as pl
from jax.experimental.pallas import tpu as pltpu

F32 = jnp.float32
BF16 = jnp.bfloat16
HIGHEST = lax.Precision.HIGHEST

EPS = 1e-6
TINY = 1e-30
NEG_BIG = -1e30
GATE_SOFTCAP = 15.0

LANES = 128
SUBLANES = 8
BF16_ROWS = 16
VMEM_LIMIT = 56 * 1024 * 1024

ROW_ALIGN = 128
TM_CAP = 1056
TN_CAP = 1024
TK_CAP = 4096
NORM_ROWS_CAP = 256
CHUNK = 64
HEADS_PER_STEP = 8
SAMPLE_ROWS = 16
ML_SAMPLE_ROWS = 8


def _largest_divisor(n, cap, mult):
    best = None
    for d in range(mult, min(n, cap) + 1, mult):
        if n % d == 0:
            best = d
    assert best is not None, (n, cap, mult)
    return best


def _round_up(n, m):
    return (n + m - 1) // m * m


def _cparams(sem):
    return pltpu.CompilerParams(dimension_semantics=sem, vmem_limit_bytes=VMEM_LIMIT)


def _dot(a, b):
    return jnp.dot(a.astype(BF16), b.astype(BF16), preferred_element_type=F32)


def _dot_nt(a, b):
    return lax.dot_general(a.astype(BF16), b.astype(BF16), (((1,), (1,)), ((), ())),
                           preferred_element_type=F32)


def _dot_tn(a, b):
    return lax.dot_general(a.astype(BF16), b.astype(BF16), (((0,), (0,)), ((), ())),
                           preferred_element_type=F32)


def _dot_hi(a, b):
    return jnp.dot(a, b, precision=HIGHEST, preferred_element_type=F32)


def _dot_nt_hi(a, b):
    return lax.dot_general(a, b, (((1,), (1,)), ((), ())), precision=HIGHEST,
                           preferred_element_type=F32)


def _sigmoid(x):
    return 1.0 / (1.0 + jnp.exp(-x))


def _silu(x):
    return x * _sigmoid(x)


def _softplus(x):
    return jnp.maximum(x, 0.0) + jnp.log1p(jnp.exp(-jnp.abs(x)))


def _log_sigmoid(x):
    return -_softplus(-x)


def _softcap(x):
    return GATE_SOFTCAP * jnp.tanh(x / GATE_SOFTCAP)


def _rms(x, g):
    return x * lax.rsqrt(jnp.mean(x * x, axis=-1, keepdims=True) + EPS) * g


def _lane_pick(x, lane_idx):
    lane = lax.broadcasted_iota(jnp.int32, x.shape, 1)
    return jnp.sum(jnp.where(lane == lane_idx, x, 0.0), axis=-1, keepdims=True)


def _tri_masks(n):
    r = lax.broadcasted_iota(jnp.int32, (n, n), 0)
    c = lax.broadcasted_iota(jnp.int32, (n, n), 1)
    return r >= c, r > c, r == c


def _cumsum_rows(col, causal_f32):
    L = col.shape[0]
    return _dot_hi(causal_f32, jnp.broadcast_to(col, (L, LANES)))


def _row_of(col_b):
    L = col_b.shape[0]
    lane = lax.broadcasted_iota(jnp.int32, (L, LANES), 1)
    e0 = (lane == 0).astype(F32)
    return _dot_nt_hi(e0, col_b)


def _transpose_rows(x):
    d = x.shape[1]
    _, _, eye = _tri_masks(d)
    return _dot_nt_hi(eye.astype(F32), x)


def _rmsnorm_kernel(x_ref, g_ref, o_ref):
    o_ref[...] = _rms(x_ref[...], g_ref[...]).astype(o_ref.dtype)


def _rmsnorm(x, g, out_dtype):
    m, d = x.shape
    tr = _largest_divisor(m, NORM_ROWS_CAP, BF16_ROWS)
    return pl.pallas_call(
        _rmsnorm_kernel,
        grid=(m // tr,),
        in_specs=[pl.BlockSpec((tr, d), lambda i: (i, 0)),
                  pl.BlockSpec((1, d), lambda i: (0, 0))],
        out_specs=pl.BlockSpec((tr, d), lambda i: (i, 0)),
        out_shape=jax.ShapeDtypeStruct((m, d), out_dtype),
        compiler_params=_cparams(("parallel",)),
        name="rmsnorm",
    )(x, g.reshape(1, d).astype(F32))


def _mm_epilogue(acc, mode, r_ref, o_ref):
    if mode == "residual":
        acc = acc + r_ref[...]
    elif mode == "relu2":
        acc = jnp.square(jnp.maximum(acc, 0.0))
    o_ref[...] = acc.astype(o_ref.dtype)


def _mm_kernel_single(*refs, mode):
    if mode == "residual":
        a_ref, w_ref, r_ref, o_ref = refs
    else:
        (a_ref, w_ref, o_ref), r_ref = refs, None
    acc = jnp.dot(a_ref[...], w_ref[...], preferred_element_type=F32)
    _mm_epilogue(acc, mode, r_ref, o_ref)


def _mm_kernel_acc(*refs, mode, nk):
    if mode == "residual":
        a_ref, w_ref, r_ref, o_ref, acc_ref = refs
    else:
        (a_ref, w_ref, o_ref, acc_ref), r_ref = refs, None
    k = pl.program_id(2)

    @pl.when(k == 0)
    def _():
        acc_ref[...] = jnp.zeros_like(acc_ref)

    acc_ref[...] += jnp.dot(a_ref[...], w_ref[...], preferred_element_type=F32)

    @pl.when(k == nk - 1)
    def _():
        _mm_epilogue(acc_ref[...], mode, r_ref, o_ref)


def _matmul(a, w, *, mode="plain", res=None, out_dtype=F32, tn_cap=TN_CAP, tk_cap=TK_CAP, name="matmul"):
    m, k = a.shape
    n = w.shape[1]
    tm = _largest_divisor(m, TM_CAP, BF16_ROWS)
    tn = _largest_divisor(n, tn_cap, LANES)
    tk = _largest_divisor(k, tk_cap, LANES)
    nk = k // tk
    args = [a, w]
    if nk == 1:
        in_specs = [pl.BlockSpec((tm, k), lambda i, j: (i, 0)),
                    pl.BlockSpec((k, tn), lambda i, j: (0, j))]
        io_spec = pl.BlockSpec((tm, tn), lambda i, j: (i, j))
        grid = (m // tm, n // tn)
        kern = functools.partial(_mm_kernel_single, mode=mode)
        scratch = []
        sem = ("parallel", "parallel")
    else:
        in_specs = [pl.BlockSpec((tm, tk), lambda i, j, kk: (i, kk)),
                    pl.BlockSpec((tk, tn), lambda i, j, kk: (kk, j))]
        io_spec = pl.BlockSpec((tm, tn), lambda i, j, kk: (i, j))
        grid = (m // tm, n // tn, nk)
        kern = functools.partial(_mm_kernel_acc, mode=mode, nk=nk)
        scratch = [pltpu.VMEM((tm, tn), F32)]
        sem = ("parallel", "parallel", "arbitrary")
    if mode == "residual":
        in_specs.append(io_spec)
        args.append(res)
    return pl.pallas_call(
        kern,
        grid=grid,
        in_specs=in_specs,
        out_specs=io_spec,
        out_shape=jax.ShapeDtypeStruct((m, n), out_dtype),
        scratch_shapes=scratch,
        compiler_params=_cparams(sem),
        name=name,
    )(*args)


def _merge_kernel(b0_ref, b1_ref, b2_ref, w_ref, g0_ref, g1_ref, g2_ref, o_ref):
    acc = None
    for n, (b_ref, g_ref) in enumerate(((b0_ref, g0_ref), (b1_ref, g1_ref), (b2_ref, g2_ref))):
        up = jnp.dot(b_ref[...], w_ref[n], preferred_element_type=F32)
        term = _sigmoid(g_ref[...]) * up
        acc = term if acc is None else acc + term
    o_ref[...] = acc.astype(o_ref.dtype)


def _merge(branches, w_br, proj, gate_off, d_model):
    m, bw = branches[0].shape
    tm = _largest_divisor(m, 768, BF16_ROWS)
    tn = _largest_divisor(math.gcd(d_model, gate_off), 512, LANES)
    g0 = gate_off // tn
    nj = d_model // tn
    br_spec = pl.BlockSpec((tm, bw), lambda i, j: (i, 0))

    def gate_spec(n):
        return pl.BlockSpec((tm, tn), lambda i, j: (i, g0 + n * nj + j))

    return pl.pallas_call(
        _merge_kernel,
        grid=(m // tm, nj),
        in_specs=[br_spec, br_spec, br_spec,
                  pl.BlockSpec((3, bw, tn), lambda i, j: (0, 0, j)),
                  gate_spec(0), gate_spec(1), gate_spec(2)],
        out_specs=pl.BlockSpec((tm, tn), lambda i, j: (i, j)),
        out_shape=jax.ShapeDtypeStruct((m, d_model), BF16),
        compiler_params=_cparams(("parallel", "parallel")),
        name="branch_merge",
    )(branches[0], branches[1], branches[2], w_br, proj, proj, proj)


def _pick_chunk(main_ref, meta_ref, is_meta, L, n_meta):
    meta = meta_ref[...]
    padded = jnp.concatenate([jnp.zeros((L - n_meta, meta.shape[1]), F32), meta], axis=0)
    return jnp.where(is_meta, padded, main_ref[...])


def _valid_rows(is_meta, L, n_meta):
    row = lax.broadcasted_iota(jnp.int32, (L, 1), 0)
    return jnp.logical_or(jnp.logical_not(is_meta), row >= L - n_meta)


def _gdn_prompt_kernel(qm_ref, km_ref, vm_ref, zm_ref, sm_ref,
                       qe_ref, ke_ref, ve_ref, ze_ref, se_ref,
                       cwq_ref, cwk_ref, cwv_ref, par_ref, ng_ref,
                       om_ref, oe_ref, st_ref,
                       s_scr, tail_scr, *, L, n_meta, n_heads, nc, hb):
    hg = pl.program_id(1)
    c = pl.program_id(2)
    is_meta = c == 0
    dk = qm_ref.shape[1] // hb

    @pl.when(is_meta)
    def _():
        s_scr[...] = jnp.zeros_like(s_scr)
        tail_scr[...] = jnp.zeros_like(tail_scr)

    valid = _valid_rows(is_meta, L, n_meta)
    x3 = jnp.concatenate([_pick_chunk(qm_ref, qe_ref, is_meta, L, n_meta),
                          _pick_chunk(km_ref, ke_ref, is_meta, L, n_meta),
                          _pick_chunk(vm_ref, ve_ref, is_meta, L, n_meta)], axis=1)
    z = _pick_chunk(zm_ref, ze_ref, is_meta, L, n_meta)
    sm = _pick_chunk(sm_ref, se_ref, is_meta, L, n_meta)

    w3 = jnp.concatenate([cwq_ref[...], cwk_ref[...], cwv_ref[...]], axis=1)
    kw = w3.shape[0]
    xp = jnp.concatenate([tail_scr[...], x3], axis=0)
    y = w3[kw - 1:kw] * x3
    for j in range(1, kw):
        y = y + w3[kw - 1 - j:kw - j] * pltpu.roll(xp, j, axis=0)[SUBLANES:SUBLANES + L]
    tail_scr[...] = x3[L - SUBLANES:L]
    y = _silu(y)
    g_all = -jnp.exp(par_ref[0:1, :]) * _softplus(sm + par_ref[1:2, :])
    b_all = _sigmoid(sm)
    causal, strict, _ = _tri_masks(L)
    causal_f = causal.astype(F32)
    ng = ng_ref[...]

    hr = range(hb)
    qs = [y[:, i * dk:(i + 1) * dk] for i in hr]
    ks = [y[:, (hb + i) * dk:(hb + i + 1) * dk] for i in hr]
    vs = [y[:, (2 * hb + i) * dk:(2 * hb + i + 1) * dk] for i in hr]
    qs = [q * lax.rsqrt(jnp.sum(q * q, axis=-1, keepdims=True) + EPS) * dk ** -0.5 for q in qs]
    ks = [k * lax.rsqrt(jnp.sum(k * k, axis=-1, keepdims=True) + EPS) for k in ks]
    gs = [jnp.where(valid, _lane_pick(g_all, hg * hb + i), 0.0) for i in hr]
    betas = [jnp.where(valid, _lane_pick(b_all, n_heads + hg * hb + i), 0.0) for i in hr]
    ss = [s_scr[i] for i in hr]

    gcs = [_cumsum_rows(g, causal_f) for g in gs]
    rows = [_row_of(gc) for gc in gcs]
    decays = [jnp.where(causal, jnp.exp(jnp.where(causal, gc[:, :L] - r, 0.0)), 0.0)
              for gc, r in zip(gcs, rows)]
    egs = [jnp.exp(gc) for gc in gcs]
    kbs = [k * b for k, b in zip(ks, betas)]
    kks = [_dot_nt(kb, k) for kb, k in zip(kbs, ks)]
    qks = [_dot_nt(q, k) for q, k in zip(qs, ks)]
    nmats = [jnp.where(strict, kk * d, 0.0) for kk, d in zip(kks, decays)]
    rhss = [jnp.concatenate([v * b, kb * eg], axis=1) for v, b, kb, eg in zip(vs, betas, kbs, egs)]

    xs = [-n for n in nmats]
    ys = [_dot_hi(n, n) for n in nmats]
    p = 2
    while True:
        xys = [_dot_hi(x, yy) for x, yy in zip(xs, ys)]
        xs = [x + yy + xy for x, yy, xy in zip(xs, ys, xys)]
        p *= 2
        if p >= L:
            break
        ys = [_dot_hi(yy, yy) for yy in ys]
    sols = [r + _dot_hi(x, r) for x, r in zip(xs, rhss)]

    wss = [_dot(sol[:, dk:], s) for sol, s in zip(sols, ss)]
    qss = [_dot(q * eg, s) for q, eg, s in zip(qs, egs, ss)]
    v_news = [sol[:, :dk] - ws for sol, ws in zip(sols, wss)]
    avs = [_dot(qk * d, vn) for qk, d, vn in zip(qks, decays, v_news)]
    g_lasts = [gc[L - 1:L, :] for gc in gcs]
    kvs = [_dot_tn(k * jnp.exp(gl - gc), vn) for k, gl, gc, vn in zip(ks, g_lasts, gcs, v_news)]
    states = [s * jnp.exp(gl) + kv for s, gl, kv in zip(ss, g_lasts, kvs)]
    for i in hr:
        s_scr[i] = states[i]
    outs = [_rms(qs_ + av, ng) * _silu(z[:, i * dk:(i + 1) * dk]) for i, (qs_, av) in enumerate(zip(qss, avs))]

    out = jnp.concatenate(outs, axis=1).astype(om_ref.dtype)

    @pl.when(is_meta)
    def _():
        oe_ref[...] = out[L - n_meta:]

    @pl.when(jnp.logical_not(is_meta))
    def _():
        om_ref[...] = out

    @pl.when(c == nc)
    def _():
        for i in range(hb):
            st_ref[0, i] = states[i]


def _prompt_specs(L, n_meta, width, col_blk, nc, meta_blk0):
    main = pl.BlockSpec((L, width), lambda b, h, c: (b * nc + jnp.maximum(c - 1, 0), col_blk + h))
    meta = pl.BlockSpec((n_meta, width), lambda b, h, c: (meta_blk0 + b, col_blk + h))
    return main, meta


def _small_specs(L, n_meta, nc, meta_blk0):
    main = pl.BlockSpec((L, LANES), lambda b, h, c: (b * nc + jnp.maximum(c - 1, 0), 0))
    meta = pl.BlockSpec((n_meta, LANES), lambda b, h, c: (meta_blk0 + b, 0))
    return main, meta


def _out_specs(L, n_meta, width, nc):
    main = pl.BlockSpec((L, width), lambda b, h, c: (b * nc + jnp.maximum(c - 1, 0), h))
    meta = pl.BlockSpec((n_meta, width), lambda b, h, c: (b, h))
    return main, meta


def _gdn_prompt(proj, small, conv_w, par, ng, lay):
    L, n_meta, B, T, H = CHUNK, lay["n_meta"], lay["B"], lay["T"], lay["gdn_heads"]
    dk = lay["gdn_dk"]
    nc = T // L
    meta_blk0 = lay["meta_row0"] // n_meta
    seg = lay["seg"]
    hb = _largest_divisor(H, HEADS_PER_STEP, 1)
    wd = hb * dk
    ng_ = H // hb
    specs_main, specs_meta = [], []
    for off in (seg["qkv"], seg["qkv"] + H * dk, seg["qkv"] + 2 * H * dk, seg["z"]):
        assert off % wd == 0
        a, b = _prompt_specs(L, n_meta, wd, off // wd, nc, meta_blk0)
        specs_main.append(a)
        specs_meta.append(b)
    sm_main, sm_meta = _small_specs(L, n_meta, nc, meta_blk0)
    kw = conv_w.shape[0]
    cw_specs = [pl.BlockSpec((kw, wd), functools.partial(lambda b, h, c, o: (0, o + h), o=o))
                for o in (0, ng_, 2 * ng_)]
    full2 = lambda b, h, c: (0, 0)
    out_main, out_meta = _out_specs(L, n_meta, wd, nc)
    kern = functools.partial(_gdn_prompt_kernel, L=L, n_meta=n_meta, n_heads=H, nc=nc, hb=hb)
    return pl.pallas_call(
        kern,
        grid=(B, ng_, nc + 1),
        in_specs=specs_main + [sm_main] + specs_meta + [sm_meta] + cw_specs
        + [pl.BlockSpec(par.shape, full2), pl.BlockSpec((1, dk), full2)],
        out_specs=[out_main, out_meta,
                   pl.BlockSpec((1, hb, dk, dk), lambda b, h, c: (b, h, 0, 0))],
        out_shape=[jax.ShapeDtypeStruct((B * T, H * dk), BF16),
                   jax.ShapeDtypeStruct((B * n_meta, H * dk), BF16),
                   jax.ShapeDtypeStruct((B, H, dk, dk), F32)],
        scratch_shapes=[pltpu.VMEM((hb, dk, dk), F32), pltpu.VMEM((SUBLANES, 3 * wd), F32)],
        compiler_params=_cparams(("parallel", "parallel", "arbitrary")),
        name="gdn_prompt",
    )(proj, proj, proj, proj, small, proj, proj, proj, proj, small,
      conv_w, conv_w, conv_w, par, ng.reshape(1, dk))


def _hgrn_level_matrix(L):
    mats = [np.tril(np.ones((L, L), np.float32))]
    bs = L
    while bs >= 2:
        half = bs // 2
        m = np.zeros((L, L), np.float32)
        for t in range(L):
            mid = (t // bs) * bs + half
            if t % bs >= half:
                m[t, mid:t + 1] = 1.0
            else:
                m[t, t + 1:mid] = 1.0
        mats.append(m)
        bs //= 2
    return np.concatenate(mats, axis=0)


def _hgrn_prompt_kernel(qm_ref, fm_ref, im_ref, gm_ref,
                        qe_ref, fe_ref, ie_ref, ge_ref,
                        lb_ref, lvl_ref, ng_ref,
                        om_ref, oe_ref, st_ref,
                        st_scr, *, L, n_meta, nc, hb):
    c = pl.program_id(2)
    is_meta = c == 0
    dk = qm_ref.shape[1] // hb
    heads = [slice(i * dk, (i + 1) * dk) for i in range(hb)]

    @pl.when(is_meta)
    def _():
        st_scr[...] = jnp.zeros_like(st_scr)

    valid = _valid_rows(is_meta, L, n_meta)
    q = _pick_chunk(qm_ref, qe_ref, is_meta, L, n_meta)
    hf = _pick_chunk(fm_ref, fe_ref, is_meta, L, n_meta)
    v = _pick_chunk(im_ref, ie_ref, is_meta, L, n_meta)
    gate = _pick_chunk(gm_ref, ge_ref, is_meta, L, n_meta)
    lb = lb_ref[...]
    f = lb + (1.0 - lb) * _sigmoid(hf)
    lf = jnp.where(valid, jnp.log(jnp.maximum(f, TINY)), 0.0)
    k = jnp.where(valid, (1.0 - lb) * _sigmoid(-hf), 0.0)

    e = _dot_hi(lvl_ref[...], lf)
    bcum = e[:L]
    row = lax.broadcasted_iota(jnp.int32, (L, 1), 0)
    r2 = lax.broadcasted_iota(jnp.int32, (L, L), 0)
    c2 = lax.broadcasted_iota(jnp.int32, (L, L), 1)
    qk = q * k
    attn = [jnp.where(r2 == c2, jnp.sum(qk[:, hs], axis=-1, keepdims=True), 0.0) for hs in heads]
    bs = L
    lev = 1
    while bs >= 2:
        half = bs // 2
        x = jnp.exp(e[lev * L:(lev + 1) * L])
        second = (row % bs) >= half
        qt = jnp.where(second, q * x, 0.0)
        kt = jnp.where(second, 0.0, k * x)
        pair = jnp.logical_and(jnp.logical_and((r2 // bs) == (c2 // bs), (r2 % bs) >= half),
                               (c2 % bs) < half)
        attn = [a + jnp.where(pair, _dot_nt(qt[:, hs], kt[:, hs]), 0.0) for a, hs in zip(attn, heads)]
        bs //= 2
        lev += 1

    qe = q * jnp.exp(bcum)
    b_last = bcum[L - 1:L, :]
    eb_last = jnp.exp(b_last)
    ke = k * jnp.exp(b_last - bcum)
    ng = ng_ref[...]
    outs, states = [], []
    for i, hs in enumerate(heads):
        st = st_scr[i]
        o = _dot_nt(qe[:, hs], st) + _dot(attn[i], v[:, hs])
        st_new = st * eb_last[:, hs] + _dot_tn(v[:, hs], ke[:, hs])
        st_scr[i] = st_new
        states.append(st_new)
        outs.append(_rms(o, ng) * _silu(gate[:, hs]))

    out = jnp.concatenate(outs, axis=1).astype(om_ref.dtype)

    @pl.when(is_meta)
    def _():
        oe_ref[...] = out[L - n_meta:]

    @pl.when(jnp.logical_not(is_meta))
    def _():
        om_ref[...] = out

    @pl.when(c == nc)
    def _():
        for i in range(hb):
            st_ref[0, i] = states[i].T


def _hgrn_prompt(proj, lb, ng, lay):
    L, n_meta, B, T, H = CHUNK, lay["n_meta"], lay["B"], lay["T"], lay["hg_heads"]
    dk = lay["hg_dk"]
    nc = T // L
    meta_blk0 = lay["meta_row0"] // n_meta
    seg = lay["seg"]
    hb = _largest_divisor(H, HEADS_PER_STEP, 1)
    wd = hb * dk
    specs_main, specs_meta = [], []
    for name in ("hq", "hf", "hi", "hgate"):
        assert seg[name] % wd == 0
        a, b = _prompt_specs(L, n_meta, wd, seg[name] // wd, nc, meta_blk0)
        specs_main.append(a)
        specs_meta.append(b)
    lvl = jnp.asarray(_hgrn_level_matrix(L))
    full2 = lambda b, h, c: (0, 0)
    out_main, out_meta = _out_specs(L, n_meta, wd, nc)
    kern = functools.partial(_hgrn_prompt_kernel, L=L, n_meta=n_meta, nc=nc, hb=hb)
    return pl.pallas_call(
        kern,
        grid=(B, H // hb, nc + 1),
        in_specs=specs_main + specs_meta
        + [pl.BlockSpec((1, wd), lambda b, h, c: (0, h)),
           pl.BlockSpec(lvl.shape, full2), pl.BlockSpec((1, dk), full2)],
        out_specs=[out_main, out_meta,
                   pl.BlockSpec((1, hb, dk, dk), lambda b, h, c: (b, h, 0, 0))],
        out_shape=[jax.ShapeDtypeStruct((B * T, H * dk), BF16),
                   jax.ShapeDtypeStruct((B * n_meta, H * dk), BF16),
                   jax.ShapeDtypeStruct((B, H, dk, dk), F32)],
        scratch_shapes=[pltpu.VMEM((hb, dk, dk), F32)],
        compiler_params=_cparams(("parallel", "parallel", "arbitrary")),
        name="hgrn_prompt",
    )(proj, proj, proj, proj, proj, proj, proj, proj, lb.reshape(1, H * dk), lvl, ng.reshape(1, dk))


def _mlstm_prompt_kernel(qm_ref, km_ref, vm_ref, om_in_ref, sm_ref,
                         qe_ref, ke_ref, ve_ref, oe_in_ref, se_ref,
                         par_ref, ng_ref,
                         om_ref, oe_ref, c_ref, n_ref, m_ref,
                         c_scr, n_scr, m_scr, *, L, n_meta, li_lane0, lf_lane0, nc):
    h = pl.program_id(1)
    c = pl.program_id(2)
    is_meta = c == 0
    dk = qm_ref.shape[1]

    @pl.when(is_meta)
    def _():
        c_scr[...] = jnp.zeros_like(c_scr)
        n_scr[...] = jnp.zeros_like(n_scr)
        m_scr[...] = jnp.zeros_like(m_scr)

    valid = _valid_rows(is_meta, L, n_meta)
    q = _pick_chunk(qm_ref, qe_ref, is_meta, L, n_meta) * dk ** -0.5
    k = _pick_chunk(km_ref, ke_ref, is_meta, L, n_meta)
    v = _pick_chunk(vm_ref, ve_ref, is_meta, L, n_meta)
    og = _pick_chunk(om_in_ref, oe_in_ref, is_meta, L, n_meta)
    sm = _pick_chunk(sm_ref, se_ref, is_meta, L, n_meta)

    li = jnp.where(valid, _lane_pick(_softcap(sm + par_ref[2:3, :]), li_lane0 + h), NEG_BIG)
    lf = jnp.where(valid, _lane_pick(_log_sigmoid(_softcap(sm + par_ref[3:4, :])), lf_lane0 + h), 0.0)

    causal, _, _ = _tri_masks(L)
    cmat = c_scr[...]
    nrow = n_scr[...]
    m_prev = m_scr[:, :1]

    bb = _cumsum_rows(lf, causal.astype(F32))
    b = bb[:, :1]
    d = bb[:, :L] - _row_of(bb - li)
    inter = b + m_prev
    m_t = jnp.maximum(inter, jnp.max(jnp.where(causal, d, NEG_BIG), axis=-1, keepdims=True))
    dw = jnp.where(causal, jnp.exp(jnp.where(causal, d - m_t, 0.0)), 0.0)
    iw = jnp.exp(inter - m_t)
    qk = _dot_nt(q, k) * dw
    num = iw * _dot(q, cmat) + _dot(qk, v)
    den = iw * jnp.sum(q * nrow, axis=-1, keepdims=True) + jnp.sum(qk, axis=-1, keepdims=True)
    hh = num / jnp.maximum(jnp.abs(den), jnp.exp(-m_t))

    b_last = b[L - 1:L, :]
    d_last = b_last - b + li
    m_new = jnp.maximum(b_last + m_prev, jnp.max(d_last, axis=0, keepdims=True))
    w_last = jnp.exp(d_last - m_new)
    dec = jnp.exp(b_last + m_prev - m_new)
    kw = k * w_last
    c_new = dec * cmat + _dot_tn(kw, v)
    n_new = dec * nrow + jnp.sum(kw, axis=0, keepdims=True)
    m_new_b = jnp.broadcast_to(m_new, m_scr.shape)
    c_scr[...] = c_new
    n_scr[...] = n_new
    m_scr[...] = m_new_b

    out = (_rms(hh, ng_ref[...]) * _sigmoid(og)).astype(om_ref.dtype)

    @pl.when(is_meta)
    def _():
        oe_ref[...] = out[L - n_meta:]

    @pl.when(jnp.logical_not(is_meta))
    def _():
        om_ref[...] = out

    @pl.when(c == nc)
    def _():
        c_ref[0, 0] = c_new
        n_ref[0, 0] = n_new
        m_ref[0, 0] = m_new_b


def _mlstm_prompt(proj, small, par, ng, lay):
    L, n_meta, B, T, H = CHUNK, lay["n_meta"], lay["B"], lay["T"], lay["ml_heads"]
    dk, dv = lay["ml_dk"], lay["ml_dv"]
    nc = T // L
    meta_blk0 = lay["meta_row0"] // n_meta
    seg = lay["seg"]
    specs_main, specs_meta = [], []
    for name, width in (("mq", dk), ("mk", dk), ("mv", dv), ("mo", dv)):
        assert seg[name] % width == 0
        a, b = _prompt_specs(L, n_meta, width, seg[name] // width, nc, meta_blk0)
        specs_main.append(a)
        specs_meta.append(b)
    sm_main, sm_meta = _small_specs(L, n_meta, nc, meta_blk0)
    full2 = lambda b, h, c: (0, 0)
    out_main, out_meta = _out_specs(L, n_meta, dv, nc)
    kern = functools.partial(_mlstm_prompt_kernel, L=L, n_meta=n_meta, nc=nc,
                             li_lane0=lay["li_lane0"], lf_lane0=lay["lf_lane0"])
    st_idx = lambda b, h, c: (b, h, 0, 0)
    return pl.pallas_call(
        kern,
        grid=(B, H, nc + 1),
        in_specs=specs_main + [sm_main] + specs_meta + [sm_meta]
        + [pl.BlockSpec(par.shape, full2), pl.BlockSpec((1, dv), full2)],
        out_specs=[out_main, out_meta,
                   pl.BlockSpec((1, 1, dk, dv), st_idx),
                   pl.BlockSpec((1, 1, 1, dk), st_idx),
                   pl.BlockSpec((1, 1, 1, LANES), st_idx)],
        out_shape=[jax.ShapeDtypeStruct((B * T, H * dv), BF16),
                   jax.ShapeDtypeStruct((B * n_meta, H * dv), BF16),
                   jax.ShapeDtypeStruct((B, H, dk, dv), F32),
                   jax.ShapeDtypeStruct((B, H, 1, dk), F32),
                   jax.ShapeDtypeStruct((B, H, 1, LANES), F32)],
        scratch_shapes=[pltpu.VMEM((dk, dv), F32), pltpu.VMEM((1, dk), F32), pltpu.VMEM((1, LANES), F32)],
        compiler_params=_cparams(("parallel", "parallel", "arbitrary")),
        name="mlstm_prompt",
    )(proj, proj, proj, proj, small, proj, proj, proj, proj, small, par, ng.reshape(1, dv))


def _gdn_sample_kernel(xq_ref, xk_ref, xv_ref, z_ref, sm_ref,
                       cq0, cq1, cq2, ck0, ck1, ck2, cv0, cv1, cv2,
                       cwq_ref, cwk_ref, cwv_ref, par_ref, ng_ref, s_ref,
                       o_ref, so_ref, *, n_heads, rows):
    h = pl.program_id(0)
    dk = xq_ref.shape[1]

    def conv(x_ref, taps, cw_ref):
        w = cw_ref[...]
        kw = w.shape[0]
        y = w[kw - 1:kw] * x_ref[...]
        for j, t_ref in enumerate(taps):
            y = y + w[j:j + 1] * t_ref[...]
        return _silu(y)

    q = conv(xq_ref, (cq0, cq1, cq2), cwq_ref)
    k = conv(xk_ref, (ck0, ck1, ck2), cwk_ref)
    v = conv(xv_ref, (cv0, cv1, cv2), cwv_ref)
    q = q * lax.rsqrt(jnp.sum(q * q, axis=-1, keepdims=True) + EPS) * dk ** -0.5
    k = k * lax.rsqrt(jnp.sum(k * k, axis=-1, keepdims=True) + EPS)
    sm = sm_ref[...]
    g = _lane_pick(-jnp.exp(par_ref[0:1, :]) * _softplus(sm + par_ref[1:2, :]), h)
    beta = _lane_pick(_sigmoid(sm), n_heads + h)
    eg = jnp.exp(g)
    q_t = _transpose_rows(q)
    k_t = _transpose_rows(k)
    outs = []
    for j in range(rows):
        s = s_ref[j]
        kc = k_t[:, j:j + 1]
        qc = q_t[:, j:j + 1]
        egj = eg[j:j + 1, :]
        ks = jnp.sum(kc * s, axis=0, keepdims=True)
        v_new = beta[j:j + 1, :] * (v[j:j + 1, :] - egj * ks)
        s_new = s * egj + kc * v_new
        so_ref[j] = s_new
        outs.append(jnp.sum(qc * s_new, axis=0, keepdims=True))
    o = jnp.concatenate(outs, axis=0)
    o_ref[...] = (_rms(o, ng_ref[...]) * _silu(z_ref[...])).astype(o_ref.dtype)


def _gdn_sample(proj, small, conv_state, state, layer, conv_w, par, ng, lay):
    S, H, dk = lay["S"], lay["gdn_heads"], lay["gdn_dk"]
    rows = SAMPLE_ROWS
    assert S % rows == 0 and lay["sample_row0"] % rows == 0
    r0 = lay["sample_row0"] // rows
    seg = lay["seg"]
    kw = conv_w.shape[0]
    ch = 3 * H * dk
    cs = conv_state.reshape(conv_state.shape[0], S, (kw - 1) * ch)
    col = {"q": seg["qkv"] // dk, "k": seg["qkv"] // dk + H, "v": seg["qkv"] // dk + 2 * H}
    x_specs = [pl.BlockSpec((rows, dk), functools.partial(lambda h, i, o: (r0 + i, o + h), o=o))
               for o in (col["q"], col["k"], col["v"], seg["z"] // dk)]
    sm_spec = pl.BlockSpec((rows, LANES), lambda h, i: (r0 + i, 0))
    tap_specs = [pl.BlockSpec((None, rows, dk),
                              functools.partial(lambda h, i, o: (layer, i, o + h), o=j * (ch // dk) + o))
                 for o in (0, H, 2 * H) for j in range(kw - 1)]
    cw_specs = [pl.BlockSpec((kw, dk), functools.partial(lambda h, i, o: (0, o + h), o=o))
                for o in (0, H, 2 * H)]
    full2 = lambda h, i: (0, 0)
    st_spec = pl.BlockSpec((None, rows, None, dk, dk), lambda h, i: (layer, i, h, 0, 0))
    kern = functools.partial(_gdn_sample_kernel, n_heads=H, rows=rows)
    return pl.pallas_call(
        kern,
        grid=(H, S // rows),
        in_specs=x_specs + [sm_spec] + tap_specs + cw_specs
        + [pl.BlockSpec(par.shape, full2), pl.BlockSpec((1, dk), full2), st_spec],
        out_specs=[pl.BlockSpec((rows, dk), lambda h, i: (i, h)),
                   pl.BlockSpec((rows, None, dk, dk), lambda h, i: (i, h, 0, 0))],
        out_shape=[jax.ShapeDtypeStruct((S, H * dk), BF16),
                   jax.ShapeDtypeStruct((S, H, dk, dk), F32)],
        compiler_params=_cparams(("parallel", "parallel")),
        name="gdn_sample",
    )(proj, proj, proj, proj, small, *([cs] * (3 * (kw - 1))), conv_w, conv_w, conv_w,
      par, ng.reshape(1, dk), state)


def _hgrn_sample_kernel(q_ref, f_ref, i_ref, g_ref, lb_ref, ng_ref, s_ref, o_ref, so_ref, *, rows):
    lb = lb_ref[...]
    hf = f_ref[...]
    f = jnp.maximum(lb + (1.0 - lb) * _sigmoid(hf), TINY)
    k = (1.0 - lb) * _sigmoid(-hf)
    v = i_ref[...]
    q_t = _transpose_rows(q_ref[...])
    k_t = _transpose_rows(k)
    f_t = _transpose_rows(f)
    outs = []
    for j in range(rows):
        s_new = f_t[:, j:j + 1] * s_ref[j] + k_t[:, j:j + 1] * v[j:j + 1, :]
        so_ref[j] = s_new
        outs.append(jnp.sum(q_t[:, j:j + 1] * s_new, axis=0, keepdims=True))
    o = jnp.concatenate(outs, axis=0)
    o_ref[...] = (_rms(o, ng_ref[...]) * _silu(g_ref[...])).astype(o_ref.dtype)


def _hgrn_sample(proj, state, layer, lb, ng, lay):
    S, H, dk = lay["S"], lay["hg_heads"], lay["hg_dk"]
    rows = SAMPLE_ROWS
    r0 = lay["sample_row0"] // rows
    seg = lay["seg"]
    x_specs = [pl.BlockSpec((rows, dk), functools.partial(lambda h, i, o: (r0 + i, o + h), o=seg[n] // dk))
               for n in ("hq", "hf", "hi", "hgate")]
    st_spec = pl.BlockSpec((None, rows, None, dk, dk), lambda h, i: (layer, i, h, 0, 0))
    return pl.pallas_call(
        functools.partial(_hgrn_sample_kernel, rows=rows),
        grid=(H, S // rows),
        in_specs=x_specs + [pl.BlockSpec((1, dk), lambda h, i: (0, h)),
                            pl.BlockSpec((1, dk), lambda h, i: (0, 0)), st_spec],
        out_specs=[pl.BlockSpec((rows, dk), lambda h, i: (i, h)),
                   pl.BlockSpec((rows, None, dk, dk), lambda h, i: (i, h, 0, 0))],
        out_shape=[jax.ShapeDtypeStruct((S, H * dk), BF16),
                   jax.ShapeDtypeStruct((S, H, dk, dk), F32)],
        compiler_params=_cparams(("parallel", "parallel")),
        name="hgrn_sample",
    )(proj, proj, proj, proj, lb.reshape(1, H * dk), ng.reshape(1, dk), state)


def _mlstm_sample_kernel(q_ref, k_ref, v_ref, og_ref, sm_ref, par_ref, ng_ref,
                         c_ref, n_ref, m_ref,
                         o_ref, co_ref, no_ref, mo_ref, *, rows, li_lane0, lf_lane0):
    h = pl.program_id(0)
    dk = q_ref.shape[1]
    q = q_ref[...] * dk ** -0.5
    k = k_ref[...]
    v = v_ref[...]
    sm = sm_ref[...]
    li = _lane_pick(_softcap(sm + par_ref[2:3, :]), li_lane0 + h)
    lf = _lane_pick(_log_sigmoid(_softcap(sm + par_ref[3:4, :])), lf_lane0 + h)
    m_prev = m_ref[...]
    m_new = jnp.maximum(lf + m_prev, li)
    dw = jnp.exp(li - m_new)
    dec = jnp.exp(lf + m_prev - m_new)
    n_new = dec * n_ref[...] + k * dw
    den = jnp.sum(q * n_new, axis=-1, keepdims=True)
    scale = 1.0 / jnp.maximum(jnp.abs(den), jnp.exp(-m_new))
    no_ref[...] = n_new
    mo_ref[...] = m_new
    q_t = _transpose_rows(q)
    k_t = _transpose_rows(k * dw)
    outs = []
    for j in range(rows):
        c_new = dec[j:j + 1, :] * c_ref[j] + k_t[:, j:j + 1] * v[j:j + 1, :]
        co_ref[j] = c_new
        outs.append(jnp.sum(q_t[:, j:j + 1] * c_new, axis=0, keepdims=True))
    hh = jnp.concatenate(outs, axis=0) * scale
    o_ref[...] = (_rms(hh, ng_ref[...]) * _sigmoid(og_ref[...])).astype(o_ref.dtype)


def _mlstm_sample(proj, small, c_state, n_state_t, m_state_t, layer, par, ng, lay):
    S, H, dk, dv = lay["S"], lay["ml_heads"], lay["ml_dk"], lay["ml_dv"]
    rows = ML_SAMPLE_ROWS
    assert S % rows == 0 and lay["sample_row0"] % rows == 0
    r0 = lay["sample_row0"] // rows
    seg = lay["seg"]
    x_specs = [pl.BlockSpec((rows, w), functools.partial(lambda h, i, o: (r0 + i, o + h), o=seg[n] // w))
               for n, w in (("mq", dk), ("mk", dk), ("mv", dv), ("mo", dv))]
    full2 = lambda h, i: (0, 0)
    kern = functools.partial(_mlstm_sample_kernel, rows=rows,
                             li_lane0=lay["li_lane0"], lf_lane0=lay["lf_lane0"])
    return pl.pallas_call(
        kern,
        grid=(H, S // rows),
        in_specs=x_specs + [pl.BlockSpec((rows, LANES), lambda h, i: (r0 + i, 0)),
                            pl.BlockSpec(par.shape, full2), pl.BlockSpec((1, dv), full2),
                            pl.BlockSpec((None, rows, None, dk, dv), lambda h, i: (layer, i, h, 0, 0)),
                            pl.BlockSpec((None, None, rows, dk), lambda h, i: (layer, h, i, 0)),
                            pl.BlockSpec((None, None, rows, 1), lambda h, i: (layer, h, i, 0))],
        out_specs=[pl.BlockSpec((rows, dv), lambda h, i: (i, h)),
                   pl.BlockSpec((rows, None, dk, dv), lambda h, i: (i, h, 0, 0)),
                   pl.BlockSpec((None, rows, dk), lambda h, i: (h, i, 0)),
                   pl.BlockSpec((None, rows, 1), lambda h, i: (h, i, 0))],
        out_shape=[jax.ShapeDtypeStruct((S, H * dv), F32),
                   jax.ShapeDtypeStruct((S, H, dk, dv), F32),
                   jax.ShapeDtypeStruct((H, S, dk), F32),
                   jax.ShapeDtypeStruct((H, S, 1), F32)],
        compiler_params=_cparams(("parallel", "parallel")),
        name="mlstm_sample",
    )(proj, proj, proj, proj, small, par, ng.reshape(1, dv), c_state, n_state_t, m_state_t)


def _layout(x_prompt, x_sample, meta_tokens, state_gdn_s, state_hg_s, state_ml_c):
    B, T, D = x_prompt.shape
    S = x_sample.shape[0] * x_sample.shape[1]
    n_meta = meta_tokens.shape[0]
    gdn_heads, gdn_dk = state_gdn_s.shape[2], state_gdn_s.shape[3]
    hg_heads, hg_dk = state_hg_s.shape[2], state_hg_s.shape[3]
    ml_heads, ml_dk, ml_dv = state_ml_c.shape[2], state_ml_c.shape[3], state_ml_c.shape[4]
    bw = gdn_heads * gdn_dk
    assert state_gdn_s.shape[4] == gdn_dk and state_hg_s.shape[4] == hg_dk
    assert hg_heads * hg_dk == bw and ml_heads * ml_dv == bw
    assert gdn_dk % LANES == 0 and hg_dk % LANES == 0 and ml_dk % LANES == 0
    assert T % CHUNK == 0 and n_meta % SUBLANES == 0 and n_meta <= CHUNK
    sizes = (("qkv", 3 * bw), ("z", bw), ("hq", bw), ("hf", bw), ("hi", bw), ("hgate", bw),
             ("mq", ml_heads * ml_dk), ("mk", ml_heads * ml_dk), ("mv", bw), ("mo", bw),
             ("gates", 3 * D))
    seg, acc = {}, 0
    for name, size in sizes:
        seg[name] = acc
        acc += size
    m = B * T + S + B * n_meta
    assert 2 * gdn_heads + 2 * ml_heads <= LANES
    assert (B * T) % SAMPLE_ROWS == 0 and (B * T + S) % n_meta == 0
    return dict(B=B, T=T, D=D, S=S, n_meta=n_meta, bw=bw, seg=seg, n_main=acc,
                gdn_heads=gdn_heads, gdn_dk=gdn_dk, hg_heads=hg_heads, hg_dk=hg_dk,
                ml_heads=ml_heads, ml_dk=ml_dk, ml_dv=ml_dv,
                rows=m, rows_padded=_round_up(m, ROW_ALIGN),
                sample_row0=B * T, meta_row0=B * T + S,
                li_lane0=2 * gdn_heads, lf_lane0=2 * gdn_heads + ml_heads)


def _split_w_in(w_in_l, lay):
    bw, gh, mh = lay["bw"], lay["gdn_heads"], lay["ml_heads"]
    mlk = mh * lay["ml_dk"]
    o_a = 4 * bw
    o_b = o_a + gh
    o_h = o_b + gh
    o_m = o_h + 4 * bw
    o_mi = o_m + 2 * mlk + 2 * bw
    o_mf = o_mi + mh
    o_g = o_mf + mh
    main = jnp.concatenate([w_in_l[:, :o_a], w_in_l[:, o_h:o_mi], w_in_l[:, o_g:]], axis=1)
    small = jnp.concatenate([w_in_l[:, o_a:o_h], w_in_l[:, o_mi:o_g]], axis=1)
    small = jnp.pad(small, ((0, 0), (0, LANES - small.shape[1])))
    return main.astype(BF16), small.astype(BF16)


def _gate_params(a_log, dt_bias, i_bias, f_bias, lay):
    gh, mh = lay["gdn_heads"], lay["ml_heads"]
    par = jnp.zeros((SUBLANES, LANES), F32)
    par = par.at[0, :gh].set(a_log.astype(F32))
    par = par.at[1, :gh].set(dt_bias.astype(F32))
    par = par.at[2, lay["li_lane0"]:lay["li_lane0"] + mh].set(i_bias.astype(F32))
    par = par.at[3, lay["lf_lane0"]:lay["lf_lane0"] + mh].set(f_bias.astype(F32))
    return par


def kernel(x_prompt, x_sample, state_gdn_s, state_gdn_conv, state_hg_s, state_ml_c, state_ml_n, state_ml_m, meta_tokens, norm1_g, w_in, gdn_conv_w, gdn_a_log, gdn_dt_bias, gdn_norm_g, hg_lb_logits, hg_norm_g, ml_i_bias, ml_f_bias, ml_norm_g, w_branch, w_out, norm2_g, w_up, w_down, final_norm_g):
    lay = _layout(x_prompt, x_sample, meta_tokens, state_gdn_s, state_hg_s, state_ml_c)
    B, T, D, S, n_meta = lay["B"], lay["T"], lay["D"], lay["S"], lay["n_meta"]
    depth = w_in.shape[0]
    kw = gdn_conv_w.shape[1]
    mp = lay["rows_padded"]
    r_s, r_m = lay["sample_row0"], lay["meta_row0"]

    h = jnp.concatenate([x_prompt.reshape(B * T, D), x_sample.reshape(S, D),
                         jnp.tile(meta_tokens.astype(x_prompt.dtype), (B, 1)),
                         jnp.zeros((mp - lay["rows"], D), x_prompt.dtype)], axis=0)

    lb_w = jax.nn.softmax(hg_lb_logits.astype(F32), axis=0)
    lb_all = jnp.cumsum(lb_w, axis=0) - lb_w[0]
    n_state_t = jnp.swapaxes(state_ml_n, 1, 2)
    m_state_t = jnp.swapaxes(state_ml_m, 1, 2)[..., None]
    pad_rows = jnp.zeros((mp - lay["rows"], lay["bw"]), BF16)

    new_p, new_s = [], []
    for l in range(depth):
        w_main, w_small = _split_w_in(w_in[l], lay)
        par = _gate_params(gdn_a_log[l], gdn_dt_bias[l], ml_i_bias[l], ml_f_bias[l], lay)

        xn = _rmsnorm(h, norm1_g[l], BF16)
        proj = _matmul(xn, w_main, name="in_proj")
        small = _matmul(xn, w_small, name="in_proj_gates")

        g_main, g_meta, p_gs = _gdn_prompt(proj, small, gdn_conv_w[l], par, gdn_norm_g[l], lay)
        h_main, h_meta, p_hs = _hgrn_prompt(proj, lb_all[l], hg_norm_g[l], lay)
        m_main, m_meta, p_c, p_n, p_m = _mlstm_prompt(proj, small, par, ml_norm_g[l], lay)

        g_smp, s_gs = _gdn_sample(proj, small, state_gdn_conv, state_gdn_s, l, gdn_conv_w[l], par,
                                  gdn_norm_g[l], lay)
        h_smp, s_hs = _hgrn_sample(proj, state_hg_s, l, lb_all[l], hg_norm_g[l], lay)
        m_smp, s_c, s_n, s_m = _mlstm_sample(proj, small, state_ml_c, n_state_t, m_state_t, l, par,
                                             ml_norm_g[l], lay)

        branches = [jnp.concatenate([a, b.astype(BF16), c, pad_rows], axis=0)
                    for a, b, c in ((g_main, g_smp, g_meta), (h_main, h_smp, h_meta), (m_main, m_smp, m_meta))]
        merged = _merge(branches, w_branch[l].astype(BF16), proj, lay["seg"]["gates"], D)
        h = _matmul(merged, w_out[l].astype(BF16), mode="residual", res=h, tn_cap=512, name="out_proj")

        xn2 = _rmsnorm(h, norm2_g[l], BF16)
        up = _matmul(xn2, w_up[l].astype(BF16), mode="relu2", out_dtype=BF16, name="mlp_up")
        h = _matmul(up, w_down[l].astype(BF16), mode="residual", res=h, tk_cap=2048, name="mlp_down")

        ch = 3 * lay["bw"]
        pre = proj[:, :ch]
        p_conv = pre[:B * T].reshape(B, T, ch)[:, T - (kw - 1):]
        s_conv = jnp.concatenate([state_gdn_conv[l][:, 1:], pre[r_s:r_s + S][:, None, :]], axis=1)
        new_p.append((p_gs, p_conv, p_hs, p_c, p_n[:, :, 0, :], p_m[:, :, 0, 0]))
        new_s.append((s_gs, s_conv, s_hs, s_c, jnp.swapaxes(s_n, 0, 1), jnp.swapaxes(s_m[..., 0], 0, 1)))

    y = _rmsnorm(h, final_norm_g, x_prompt.dtype)
    y_prompt = y[:B * T].reshape(B, T, D)
    y_sample = y[r_s:r_s + S].reshape(x_sample.shape)
    stack = lambda per_layer: tuple(jnp.stack([st[i] for st in per_layer], axis=0) for i in range(6))
    return (y_prompt, y_sample) + stack(new_p) + stack(new_s)
```
